```python
import math, functools
import numpy as np
import jax
import jax.numpy as jnp
from jax import lax

D_MODEL = 2048
BATCH = 2
SEQ = 4096
DEPTH = 2
DEC_BATCH = 128
DEC_SEQ = 4
PAST_LEN = 8192
PAGE_SIZE = 128

MLA_HEADS = 16
QK_NOPE = 128
QK_ROPE = 64
V_HEAD = 128
KV_RANK = 512
ROPE_THETA = 10000.0
MLA_OUT = MLA_HEADS * V_HEAD
ATTN_SCALE = (QK_NOPE + QK_ROPE) ** -0.5
Q_BLOCK = 128

POOL_WINDOWS = (2, 4, 8, 16)
POOL_W = 2048
POOL_GROUP = POOL_W // len(POOL_WINDOWS)
POOL_CTX = max(POOL_WINDOWS) - 1

MLSTM_HEADS = 8
MLSTM_DH = 256
MLSTM_W = MLSTM_HEADS * MLSTM_DH
MLSTM_CHUNK = 64

N_BRANCH = 3
D_FF = 4 * D_MODEL
EPS = 1e-6

Q_COLS = MLA_HEADS * (QK_NOPE + QK_ROPE)
KV_COLS = KV_RANK + QK_ROPE
IF_COLS = 2 * MLSTM_HEADS
GATE_COLS = N_BRANCH * D_MODEL
IN_COLS = Q_COLS + KV_COLS + POOL_W + 4 * MLSTM_W + IF_COLS + GATE_COLS

kernel_name = "hybrid_mla_pool_mlstm_decoder_step"

F32 = jnp.float32


def rms_norm(x, g):
    xf = x.astype(F32)
    return xf * lax.rsqrt(jnp.mean(xf * xf, axis=-1, keepdims=True) + EPS) * g.astype(F32)


def rope_tables(pos):
    inv = ROPE_THETA ** (-jnp.arange(0, QK_ROPE, 2, dtype=F32) / QK_ROPE)
    ang = pos[:, None] * inv[None, :]
    return jnp.cos(ang), jnp.sin(ang)


def apply_rope(x, cos, sin):
    x1, x2 = jnp.split(x, 2, axis=-1)
    return jnp.concatenate([x1 * cos - x2 * sin, x2 * cos + x1 * sin], axis=-1)


def latent_scores(q_lat, q_pe, k_lat, k_pe):
    s = jnp.einsum('bqhc,bkc->bhqk', q_lat, k_lat) + jnp.einsum('bqhr,bkr->bhqk', q_pe, k_pe)
    return s.astype(F32) * ATTN_SCALE


def mla_prompt(q_lat, q_pe, c_kv, k_pe):
    b, s, h, c = q_lat.shape
    qb = Q_BLOCK if s % Q_BLOCK == 0 else s
    nb = s // qb
    ql = jnp.moveaxis(q_lat.reshape(b, nb, qb, h, c), 1, 0)
    qp = jnp.moveaxis(q_pe.reshape(b, nb, qb, h, QK_ROPE), 1, 0)
    kpos = jnp.arange(s)

    def block(args):
        ql_i, qp_i, i = args
        sc = latent_scores(ql_i, qp_i, c_kv, k_pe)
        qpos = i * qb + jnp.arange(qb)
        sc = jnp.where(kpos[None, :] <= qpos[:, None], sc, -jnp.inf)
        p = jax.nn.softmax(sc, axis=-1)
        return jnp.einsum('bhqk,bkc->bqhc', p, c_kv)

    o = lax.map(block, (ql, qp, jnp.arange(nb)))
    return jnp.moveaxis(o, 0, 1).reshape(b, s, h, c)


def online_update(carry, sc, v):
    m, l, acc = carry
    m_new = jnp.maximum(m, sc.max(-1))
    corr = jnp.exp(m - m_new)
    p = jnp.exp(sc - m_new[..., None])
    l = l * corr + p.sum(-1)
    acc = acc * jnp.swapaxes(corr, 1, 2)[..., None] + jnp.einsum('bhqk,bkc->bqhc', p, v)
    return m_new, l, acc


def mla_sample(q_lat, q_pe, c_new, kpe_new, cache_latent, cache_krope, page_table, layer):
    b, t, h, c = q_lat.shape
    init = (jnp.full((b, h, t), -jnp.inf, F32), jnp.zeros((b, h, t), F32), jnp.zeros((b, t, h, c), F32))

    def page_step(carry, pages):
        k_lat = cache_latent[layer, pages]
        k_pe = cache_krope[layer, pages]
        return online_update(carry, latent_scores(q_lat, q_pe, k_lat, k_pe), k_lat), None

    carry, _ = lax.scan(page_step, init, page_table.T)
    causal = jnp.tril(jnp.ones((t, t), bool))
    sc = jnp.where(causal, latent_scores(q_lat, q_pe, c_new, kpe_new), -jnp.inf)
    _, l, acc = online_update(carry, sc, c_new)
    return acc / jnp.swapaxes(l, 1, 2)[..., None]


def pool_mixer(u_ctx, pos0, n_out, maps, scale):
    uf = u_ctx.astype(F32)
    tc = uf.shape[1]
    cs = jnp.cumsum(uf, axis=1)
    pos = pos0 + jnp.arange(tc)
    outs = []
    for g, w in enumerate(POOL_WINDOWS):
        sl = slice(g * POOL_GROUP, (g + 1) * POOL_GROUP)
        cg = cs[..., sl]
        lagged = jnp.pad(cg, ((0, 0), (w, 0), (0, 0)))[:, :tc]
        cnt = jnp.minimum(w, pos + 1).astype(F32)
        d = (cg - lagged) / cnt[None, :, None] - uf[..., sl]
        outs.append(d[:, tc - n_out:] @ maps[g])
    return jnp.concatenate(outs, axis=-1) * scale


def mlstm_chunkwise(q, k, v, logi, logf, C0, n0, m0):
    b, h, t, d = q.shape
    L = math.gcd(t, MLSTM_CHUNK)
    nc = t // L

    def chunks(a):
        return jnp.moveaxis(a.reshape(a.shape[:2] + (nc, L) + a.shape[3:]), 2, 0)

    causal = jnp.tril(jnp.ones((L, L), bool))

    def step(carry, xs):
        C, n, m = carry
        qc, kc, vc, li, lf = xs
        bcum = jnp.cumsum(lf, axis=-1)
        dmat = jnp.where(causal, bcum[..., :, None] - bcum[..., None, :] + li[..., None, :], -jnp.inf)
        m_t = jnp.maximum(bcum + m[..., None], dmat.max(-1))
        inter = jnp.exp(bcum + m[..., None] - m_t)
        wts = jnp.exp(dmat - m_t[..., None])
        s = jnp.einsum('bhtd,bhsd->bhts', qc, kc) * wts
        num = inter[..., None] * jnp.einsum('bhtd,bhde->bhte', qc, C) + jnp.einsum('bhts,bhse->bhte', s, vc)
        den = inter * jnp.einsum('bhtd,bhd->bht', qc, n) + s.sum(-1)
        h_out = num / jnp.maximum(jnp.abs(den), jnp.exp(-m_t))[..., None]
        w_last = wts[..., -1, :]
        decay = inter[..., -1]
        C_new = decay[..., None, None] * C + jnp.einsum('bhsd,bhse->bhde', kc * w_last[..., None], vc)
        n_new = decay[..., None] * n + jnp.einsum('bhs,bhsd->bhd', w_last, kc)
        return (C_new, n_new, m_t[..., -1]), h_out

    (C1, n1, m1), hs = lax.scan(step, (C0, n0, m0), tuple(chunks(a) for a in (q, k, v, logi, logf)))
    hs = jnp.moveaxis(hs, 0, 2).reshape(b, h, t, d)
    return hs, (C1, n1, m1)


def hybrid_layer(x, l, pos0, weights, attend, pool_prev, m_prev):
    (g_mix, w_in, kv_norm_g, w_ukv, w_pool, pool_scale, b_i, b_f, mlstm_norm_g,
     w_proj_attn, w_proj_pool, w_proj_mlstm, w_out, g_mlp, w_up, w_down) = weights
    b, t, _ = x.shape
    h = rms_norm(x, g_mix[l])
    z = h @ w_in[l]
    split_idx = np.cumsum((Q_COLS, KV_COLS, POOL_W, MLSTM_W, MLSTM_W, MLSTM_W, MLSTM_W, IF_COLS)).tolist()
    zq, zkv, u, mq, mk, mv, mo, zif, zg = jnp.split(z, split_idx, axis=-1)
    cos, sin = rope_tables(pos0 + jnp.arange(t, dtype=F32))

    q = zq.reshape(b, t, MLA_HEADS, QK_NOPE + QK_ROPE)
    q_nope = q[..., :QK_NOPE]
    q_pe = apply_rope(q[..., QK_NOPE:], cos[:, None, :], sin[:, None, :])
    c_kv = rms_norm(zkv[..., :KV_RANK], kv_norm_g[l])
    k_pe = apply_rope(zkv[..., KV_RANK:], cos, sin)
    w_uk = w_ukv[l][..., :QK_NOPE]
    w_uv = w_ukv[l][..., QK_NOPE:]
    q_lat = jnp.einsum('bthd,chd->bthc', q_nope, w_uk)
    o_lat = attend(q_lat, q_pe, c_kv, k_pe)
    o_attn = jnp.einsum('bthc,chd->bthd', o_lat, w_uv).reshape(b, t, MLA_OUT)
    br_a = o_attn @ w_proj_attn[l]

    if pool_prev is None:
        u_ctx, p0 = u, pos0
    else:
        u_ctx, p0 = jnp.concatenate([pool_prev.astype(F32), u], axis=1), pos0 - POOL_CTX
    br_b = pool_mixer(u_ctx, p0, t, w_pool[l], pool_scale[l]) @ w_proj_pool[l]
    new_pool = u_ctx[:, -POOL_CTX:]

    def heads(a):
        return a.reshape(b, t, MLSTM_HEADS, MLSTM_DH).transpose(0, 2, 1, 3)
    logi = (zif[..., :MLSTM_HEADS] + b_i[l]).transpose(0, 2, 1)
    logf = jax.nn.log_sigmoid(zif[..., MLSTM_HEADS:] + b_f[l]).transpose(0, 2, 1)
    if m_prev is None:
        C0 = jnp.zeros((b, MLSTM_HEADS, MLSTM_DH, MLSTM_DH), F32)
        n0 = jnp.zeros((b, MLSTM_HEADS, MLSTM_DH), F32)
        m0 = jnp.zeros((b, MLSTM_HEADS), F32)
    else:
        C0, n0, m0 = (a.astype(F32) for a in m_prev)
    hm, (C1, n1, m1) = mlstm_chunkwise(heads(mq), heads(mk) * (MLSTM_DH ** -0.5), heads(mv), logi, logf, C0, n0, m0)
    hm = rms_norm(hm, mlstm_norm_g[l].reshape(MLSTM_HEADS, 1, MLSTM_DH))
    hm = hm.transpose(0, 2, 1, 3).reshape(b, t, MLSTM_W) * jax.nn.sigmoid(mo)
    br_c = hm @ w_proj_mlstm[l]

    gates = jax.nn.sigmoid(zg.reshape(b, t, N_BRANCH, D_MODEL))
    merged = gates[:, :, 0] * br_a + gates[:, :, 1] * br_b + gates[:, :, 2] * br_c
    x = x + merged @ w_out[l]

    hf = rms_norm(x, g_mlp[l])
    x = x + jnp.square(jax.nn.relu(hf @ w_up[l])) @ w_down[l]
    return x, (c_kv, k_pe, C1, n1, m1, new_pool)


def setup_inputs(seed: int = 0) -> dict:
    key = jax.random.key(seed)
    ks = jax.random.split(key, 32)
    nrm = jax.random.normal
    n_pages = PAST_LEN // PAGE_SIZE
    used = DEC_BATCH * n_pages
    n_pool_pages = used + used // 4
    perm = jax.random.permutation(ks[4], n_pool_pages).astype(jnp.int32)
    page_table = perm[:used].reshape(DEC_BATCH, n_pages)
    return {
        "x_prompt": nrm(ks[0], (BATCH, SEQ, D_MODEL), F32),
        "x_sample": nrm(ks[1], (DEC_BATCH, DEC_SEQ, D_MODEL), F32),
        "cache_latent": nrm(ks[2], (DEPTH, n_pool_pages, PAGE_SIZE, KV_RANK), F32),
        "cache_krope": nrm(ks[3], (DEPTH, n_pool_pages, PAGE_SIZE, QK_ROPE), F32),
        "page_table": page_table,
        "state_C": 0.1 * nrm(ks[5], (DEPTH, DEC_BATCH, MLSTM_HEADS, MLSTM_DH, MLSTM_DH), F32),
        "state_n": 0.1 * nrm(ks[6], (DEPTH, DEC_BATCH, MLSTM_HEADS, MLSTM_DH), F32),
        "state_m": nrm(ks[7], (DEPTH, DEC_BATCH, MLSTM_HEADS), F32),
        "state_pool": nrm(ks[8], (DEPTH, DEC_BATCH, POOL_CTX, POOL_W), F32),
        "g_mix": 1.0 + 0.02 * nrm(ks[9], (DEPTH, D_MODEL), F32),
        "w_in": nrm(ks[10], (DEPTH, D_MODEL, IN_COLS), F32) * D_MODEL ** -0.5,
        "kv_norm_g": 1.0 + 0.02 * nrm(ks[11], (DEPTH, KV_RANK), F32),
        "w_ukv": nrm(ks[12], (DEPTH, KV_RANK, MLA_HEADS, QK_NOPE + V_HEAD), F32) * KV_RANK ** -0.5,
        "w_pool": nrm(ks[13], (DEPTH, len(POOL_WINDOWS), POOL_GROUP, POOL_GROUP), F32) * POOL_GROUP ** -0.5,
        "pool_scale": 1.0 + 0.02 * nrm(ks[14], (DEPTH, POOL_W), F32),
        "b_i": 0.1 * nrm(ks[15], (DEPTH, MLSTM_HEADS), F32),
        "b_f": jnp.linspace(3.0, 6.0, MLSTM_HEADS, dtype=F32)[None, :] + 0.1 * nrm(ks[16], (DEPTH, MLSTM_HEADS), F32),
        "mlstm_norm_g": 1.0 + 0.02 * nrm(ks[17], (DEPTH, MLSTM_W), F32),
        "w_proj_attn": nrm(ks[18], (DEPTH, MLA_OUT, D_MODEL), F32) * MLA_OUT ** -0.5,
        "w_proj_pool": nrm(ks[19], (DEPTH, POOL_W, D_MODEL), F32) * POOL_W ** -0.5,
        "w_proj_mlstm": nrm(ks[20], (DEPTH, MLSTM_W, D_MODEL), F32) * MLSTM_W ** -0.5,
        "w_out": nrm(ks[21], (DEPTH, D_MODEL, D_MODEL), F32) * D_MODEL ** -0.5,
        "g_mlp": 1.0 + 0.02 * nrm(ks[22], (DEPTH, D_MODEL), F32),
        "w_up": nrm(ks[23], (DEPTH, D_MODEL, D_FF), F32) * D_MODEL ** -0.5,
        "w_down": nrm(ks[24], (DEPTH, D_FF, D_MODEL), F32) * D_FF ** -0.5,
        "g_final": 1.0 + 0.02 * nrm(ks[25], (D_MODEL,), F32),
    }


def reference(x_prompt, x_sample, cache_latent, cache_krope, page_table, state_C, state_n, state_m, state_pool,
              g_mix, w_in, kv_norm_g, w_ukv, w_pool, pool_scale, b_i, b_f, mlstm_norm_g,
              w_proj_attn, w_proj_pool, w_proj_mlstm, w_out, g_mlp, w_up, w_down, g_final):
    weights = (g_mix, w_in, kv_norm_g, w_ukv, w_pool, pool_scale, b_i, b_f, mlstm_norm_g,
               w_proj_attn, w_proj_pool, w_proj_mlstm, w_out, g_mlp, w_up, w_down)
    xp = x_prompt.astype(F32)
    xs = x_sample.astype(F32)
    p_states, s_states = [], []
    for l in range(DEPTH):
        xp, st = hybrid_layer(xp, l, 0, weights, mla_prompt, None, None)
        p_states.append(st)
        attend = functools.partial(mla_sample, cache_latent=cache_latent, cache_krope=cache_krope,
                                   page_table=page_table, layer=l)
        xs, st = hybrid_layer(xs, l, PAST_LEN, weights, attend, state_pool[l], (state_C[l], state_n[l], state_m[l]))
        s_states.append(st)

    def stack(states, i, like):
        return jnp.stack([s[i] for s in states]).astype(like.dtype)

    y_prompt = rms_norm(xp, g_final).astype(x_prompt.dtype)
    y_sample = rms_norm(xs, g_final).astype(x_sample.dtype)
    p_lat = stack(p_states, 0, cache_latent)
    p_rope = stack(p_states, 1, cache_krope)
    p_C = stack(p_states, 2, state_C)
    p_n = stack(p_states, 3, state_n)
    p_m = stack(p_states, 4, state_m)
    p_pool = stack(p_states, 5, state_pool)
    s_lat = stack(s_states, 0, cache_latent)
    s_rope = stack(s_states, 1, cache_krope)
    s_C = stack(s_states, 2, state_C)
    s_n = stack(s_states, 3, state_n)
    s_m = stack(s_states, 4, state_m)
    s_pool = stack(s_states, 5, state_pool)
    return (y_prompt, y_sample, p_lat, p_rope, p_C, p_n, p_m, p_pool, s_lat, s_rope, s_C, s_n, s_m, s_pool)
```

```python
import functools
import math

import numpy as np
import jax
import jax.numpy as jnp
from jax import lax
from jax.experimental import pallas as pl
from jax.experimental.pallas import tpu as pltpu

D_MODEL = 2048
BATCH = 2
SEQ = 4096
DEPTH = 2
DEC_BATCH = 128
DEC_SEQ = 4
PAST_LEN = 8192
PAGE_SIZE = 128
N_PAGES = PAST_LEN // PAGE_SIZE

MLA_HEADS = 16
QK_NOPE = 128
QK_ROPE = 64
V_HEAD = 128
KV_RANK = 512
ROPE_THETA = 10000.0
ATTN_SCALE = (QK_NOPE + QK_ROPE) ** -0.5
Q_COLS = MLA_HEADS * (QK_NOPE + QK_ROPE)
KV_COLS = KV_RANK + QK_ROPE

POOL_WINDOWS = (2, 4, 8, 16)
POOL_W = 2048
POOL_GROUP = POOL_W // len(POOL_WINDOWS)
POOL_CTX = max(POOL_WINDOWS) - 1

MLSTM_HEADS = 8
MLSTM_DH = 256
MLSTM_W = MLSTM_HEADS * MLSTM_DH
N_BRANCH = 3
D_FF = 4 * D_MODEL
EPS = 1e-6

NP = BATCH * SEQ
NS = DEC_BATCH * DEC_SEQ
NT = NP + NS

LANE = 128
QH = 2 * LANE
KVP = KV_RANK + LANE
NEG = -1e30
VMEM_LIMIT = 56 * 1024 * 1024

F32 = jnp.float32
BF16 = jnp.bfloat16


def _params(sem, vmem=VMEM_LIMIT):
    return pltpu.CompilerParams(dimension_semantics=sem, vmem_limit_bytes=vmem)


def _nt_dot(a, b):
    return lax.dot_general(a, b, (((1,), (1,)), ((), ())), preferred_element_type=F32)


def _tn_dot(a, b):
    return lax.dot_general(a, b, (((0,), (0,)), ((), ())), preferred_element_type=F32)


def _rmsnorm_kernel(x_ref, g_ref, o_ref):
    x = x_ref[...]
    ms = jnp.mean(x * x, axis=-1, keepdims=True)
    o_ref[...] = (x * lax.rsqrt(ms + EPS) * g_ref[...]).astype(o_ref.dtype)


def _rmsnorm(x, g, out_dtype, tm=544):
    m, d = x.shape
    return pl.pallas_call(
        _rmsnorm_kernel,
        grid=(m // tm,),
        in_specs=[pl.BlockSpec((tm, d), lambda i: (i, 0)), pl.BlockSpec((1, d), lambda i: (0, 0))],
        out_specs=pl.BlockSpec((tm, d), lambda i: (i, 0)),
        out_shape=jax.ShapeDtypeStruct((m, d), out_dtype),
        compiler_params=_params(("parallel",)),
        name="rmsnorm",
    )(x, g.reshape(1, d))


def _matmul_kernel(*refs, n_extra, nk, epilogue):
    x_ref, w_ref = refs[0], refs[1]
    extra = refs[2:2 + n_extra]
    o_ref = refs[2 + n_extra]
    part = jnp.dot(x_ref[...].astype(BF16), w_ref[...], preferred_element_type=F32)
    if nk == 1:
        o_ref[...] = epilogue(part, *extra).astype(o_ref.dtype)
        return
    acc_ref = refs[3 + n_extra]
    k = pl.program_id(2)

    @pl.when(k == 0)
    def _():
        acc_ref[...] = part

    @pl.when(k > 0)
    def _():
        acc_ref[...] += part

    @pl.when(k == nk - 1)
    def _():
        o_ref[...] = epilogue(acc_ref[...], *extra).astype(o_ref.dtype)


def _matmul(x, w, *, tm, tn, out_dtype, name, tk=None, rows=None, epilogue=None, extras=()):
    m = rows or x.shape[0]
    kdim, n = w.shape
    tk = tk or kdim
    nk = kdim // tk
    if epilogue is None:
        epilogue = lambda acc: acc
    kern = functools.partial(_matmul_kernel, n_extra=len(extras), nk=nk, epilogue=epilogue)
    return pl.pallas_call(
        kern,
        grid=(m // tm, n // tn, nk),
        in_specs=[pl.BlockSpec((tm, tk), lambda i, j, k: (i, k)),
                  pl.BlockSpec((tk, tn), lambda i, j, k: (k, j))] + [s for _, s in extras],
        out_specs=pl.BlockSpec((tm, tn), lambda i, j, k: (i, j)),
        out_shape=jax.ShapeDtypeStruct((m, n), out_dtype),
        scratch_shapes=[pltpu.VMEM((tm, tn), F32)] if nk > 1 else [],
        compiler_params=_params(("parallel", "parallel", "arbitrary")),
        name=name,
    )(x, w, *[a for a, _ in extras])


def _blockdiag_kernel(x_ref, w_ref, o_ref):
    o_ref[...] = jnp.dot(x_ref[...], w_ref[...], preferred_element_type=F32).astype(o_ref.dtype)


def _blockdiag_matmul(x, w, out_dtype, name):
    m = x.shape[0]
    g, kg, ng = w.shape
    return pl.pallas_call(
        _blockdiag_kernel,
        grid=(g,),
        in_specs=[pl.BlockSpec((m, kg), lambda i: (0, i)), pl.BlockSpec((None, kg, ng), lambda i: (i, 0, 0))],
        out_specs=pl.BlockSpec((m, ng), lambda i: (0, i)),
        out_shape=jax.ShapeDtypeStruct((m, g * ng), out_dtype),
        compiler_params=_params(("parallel",)),
        name=name,
    )(x, w)


def _rope_lanes(pe, c_ref, s1_ref, s2_ref):
    return (pe * c_ref[...] + pltpu.roll(pe, LANE - QK_ROPE // 2, 1) * s1_ref[...]
            + pltpu.roll(pe, QK_ROPE // 2, 1) * s2_ref[...])


def _rope_tables():
    pos = jnp.concatenate([jnp.tile(jnp.arange(SEQ, dtype=F32), BATCH),
                           jnp.tile(PAST_LEN + jnp.arange(DEC_SEQ, dtype=F32), DEC_BATCH)])
    inv = ROPE_THETA ** (-jnp.arange(0, QK_ROPE, 2, dtype=F32) / QK_ROPE)
    ang = pos[:, None] * inv[None, :]
    cos, sin = jnp.cos(ang), jnp.sin(ang)
    z = jnp.zeros_like(cos)
    c = jnp.concatenate([cos, cos, z, z], axis=1)
    s1 = jnp.concatenate([-sin, z, z, z], axis=1)
    s2 = jnp.concatenate([z, sin, z, z], axis=1)
    return c, s1, s2


def _q_epilogue(acc, c_ref, s1_ref, s2_ref):
    parts = []
    for h in range(acc.shape[1] // QH):
        parts.append(acc[:, h * QH:h * QH + LANE])
        parts.append(_rope_lanes(acc[:, h * QH + LANE:(h + 1) * QH], c_ref, s1_ref, s2_ref))
    return jnp.concatenate(parts, axis=1) * ATTN_SCALE


def _kv_epilogue(acc, g_ref, c_ref, s1_ref, s2_ref):
    lat = acc[:, :KV_RANK]
    ms = jnp.mean(lat * lat, axis=-1, keepdims=True)
    lat = lat * lax.rsqrt(ms + EPS) * g_ref[...]
    return jnp.concatenate([lat, _rope_lanes(acc[:, KV_RANK:], c_ref, s1_ref, s2_ref)], axis=1)


def _attn_kernel(q_ref, k_ref, v_ref, o_ref, m_ref, l_ref, acc_ref, *, tq):
    iq = pl.program_id(2)
    q = q_ref[...]
    m_ref[...] = jnp.full_like(m_ref, NEG)
    l_ref[...] = jnp.zeros_like(l_ref)
    acc_ref[...] = jnp.zeros_like(acc_ref)

    def step(ik, diagonal):
        start = pl.multiple_of(ik * tq, tq)
        k = k_ref[pl.ds(start, tq), :]
        v = v_ref[pl.ds(start, tq), :]
        s = _nt_dot(q, k)
        if diagonal:
            ri = lax.broadcasted_iota(jnp.int32, s.shape, 0)
            ci = lax.broadcasted_iota(jnp.int32, s.shape, 1)
            s = jnp.where(ci <= ri, s, NEG)
        m_prev = m_ref[...]
        m_new = jnp.maximum(m_prev, jnp.max(s, axis=-1, keepdims=True))
        alpha = jnp.exp(m_prev - m_new)
        p = jnp.exp(s - m_new)
        l_ref[...] = alpha * l_ref[...] + jnp.sum(p, axis=-1, keepdims=True)
        acc_ref[...] = alpha * acc_ref[...] + jnp.dot(p.astype(BF16), v, preferred_element_type=F32)
        m_ref[...] = m_new

    def body(ik, carry):
        step(ik, False)
        return carry

    lax.fori_loop(0, iq, body, 0)
    step(iq, True)
    o_ref[...] = (acc_ref[...] / l_ref[...]).astype(o_ref.dtype)


def _prompt_attention(q, kv, tq=512):
    nq = SEQ // tq
    v_off = MLA_HEADS * QH // V_HEAD
    return pl.pallas_call(
        functools.partial(_attn_kernel, tq=tq),
        grid=(BATCH, MLA_HEADS, nq),
        in_specs=[pl.BlockSpec((tq, QH), lambda b, h, i: (b * nq + i, h)),
                  pl.BlockSpec((SEQ, QH), lambda b, h, i: (b, h)),
                  pl.BlockSpec((SEQ, V_HEAD), lambda b, h, i: (b, v_off + h))],
        out_specs=pl.BlockSpec((tq, V_HEAD), lambda b, h, i: (b * nq + i, h)),
        out_shape=jax.ShapeDtypeStruct((NP, MLA_HEADS * V_HEAD), BF16),
        scratch_shapes=[pltpu.VMEM((tq, 1), F32), pltpu.VMEM((tq, 1), F32), pltpu.VMEM((tq, V_HEAD), F32)],
        compiler_params=_params(("parallel", "parallel", "arbitrary")),
        name="prompt_attention",
    )(q, kv, kv)


def _decode_kernel(pt_ref, q_ref, *refs, pg, n_groups):
    lat_refs, kr_refs = refs[:pg], refs[pg:2 * pg]
    new_ref, o_ref, m_ref, l_ref, acc_ref = refs[2 * pg:]
    g = pl.program_id(1)

    @pl.when(g == 0)
    def _():
        m_ref[...] = jnp.full_like(m_ref, NEG)
        l_ref[...] = jnp.zeros_like(l_ref)
        acc_ref[...] = jnp.zeros_like(acc_ref)

    q = q_ref[...]
    lat = jnp.concatenate([r[...].astype(BF16) for r in lat_refs], axis=0)
    kr = jnp.concatenate([r[...] for r in kr_refs], axis=0)
    kr = jnp.concatenate([kr, jnp.zeros_like(kr)], axis=1).astype(BF16)
    s = _nt_dot(q[:, :KV_RANK], lat) + _nt_dot(q[:, KV_RANK:], kr)
    m_prev = m_ref[...]
    m_new = jnp.maximum(m_prev, jnp.max(s, axis=-1, keepdims=True))
    alpha = jnp.exp(m_prev - m_new)
    p = jnp.exp(s - m_new)
    l_ref[...] = alpha * l_ref[...] + jnp.sum(p, axis=-1, keepdims=True)
    acc_ref[...] = alpha * acc_ref[...] + jnp.dot(p.astype(BF16), lat, preferred_element_type=F32)
    m_ref[...] = m_new

    @pl.when(g == n_groups - 1)
    def _():
        qf = q.astype(F32)
        row = lax.broadcasted_iota(jnp.int32, m_ref.shape, 0)
        m_run, l_run, acc = m_ref[...], l_ref[...], acc_ref[...]
        for t in range(DEC_SEQ):
            kn = new_ref[t:t + 1, :]
            st = jnp.sum(qf * kn, axis=-1, keepdims=True)
            st = jnp.where(row >= t * MLA_HEADS, st, NEG)
            m_nxt = jnp.maximum(m_run, st)
            a = jnp.exp(m_run - m_nxt)
            pt = jnp.exp(st - m_nxt)
            l_run = a * l_run + pt
            acc = a * acc + pt * kn[:, :KV_RANK]
            m_run = m_nxt
        o_ref[...] = (acc / l_run).astype(o_ref.dtype)


def _sample_attention(q_abs, ckv_new, cache_latent, cache_krope, page_table, layer, pg=16):
    n_groups = N_PAGES // pg
    rows = DEC_SEQ * MLA_HEADS

    def page_spec(width, i):
        return pl.BlockSpec((None, None, PAGE_SIZE, width),
                            lambda b, g, pt: (layer, pt[b * N_PAGES + g * pg + i], 0, 0))

    grid_spec = pltpu.PrefetchScalarGridSpec(
        num_scalar_prefetch=1,
        grid=(DEC_BATCH, n_groups),
        in_specs=[pl.BlockSpec((None, rows, KVP), lambda b, g, pt: (b, 0, 0))]
        + [page_spec(KV_RANK, i) for i in range(pg)]
        + [page_spec(QK_ROPE, i) for i in range(pg)]
        + [pl.BlockSpec((None, DEC_SEQ, KVP), lambda b, g, pt: (b, 0, 0))],
        out_specs=pl.BlockSpec((None, rows, KV_RANK), lambda b, g, pt: (b, 0, 0)),
        scratch_shapes=[pltpu.VMEM((rows, 1), F32), pltpu.VMEM((rows, 1), F32), pltpu.VMEM((rows, KV_RANK), F32)],
    )
    return pl.pallas_call(
        functools.partial(_decode_kernel, pg=pg, n_groups=n_groups),
        grid_spec=grid_spec,
        out_shape=jax.ShapeDtypeStruct((DEC_BATCH, rows, KV_RANK), BF16),
        compiler_params=_params(("parallel", "arbitrary")),
        name="sample_attention",
    )(page_table.reshape(-1), q_abs, *([cache_latent] * pg), *([cache_krope] * pg), ckv_new)


def _pool_prompt_kernel(u_ref, halo_ref, maps_ref, scale_ref, o_ref, *, tm):
    row0 = lax.rem(pl.program_id(0) * tm, SEQ)
    pos1 = row0 + 1 + lax.broadcasted_iota(jnp.int32, (tm, 1), 0)
    for g, w in enumerate(POOL_WINDOWS):
        sl = slice(g * POOL_GROUP, (g + 1) * POOL_GROUP)
        u = u_ref[:, sl]
        halo = jnp.where(row0 != 0, halo_ref[:, sl], 0.0)
        acc = jnp.concatenate([halo, u], axis=0)
        shift = 1
        while shift < w:
            acc = acc + pltpu.roll(acc, shift, 0)
            shift *= 2
        cnt = jnp.minimum(w, pos1).astype(F32)
        d = acc[POOL_CTX + 1:, :] / cnt - u
        y = jnp.dot(d.astype(BF16), maps_ref[g], preferred_element_type=F32) * scale_ref[:, sl]
        o_ref[:, sl] = y.astype(o_ref.dtype)


def _pool_prompt(u, maps, scale, tm=512):
    hb = POOL_CTX + 1
    return pl.pallas_call(
        functools.partial(_pool_prompt_kernel, tm=tm),
        grid=(NP // tm,),
        in_specs=[pl.BlockSpec((tm, POOL_W), lambda i: (i, 0)),
                  pl.BlockSpec((hb, POOL_W), lambda i: (jnp.maximum(i * (tm // hb) - 1, 0), 0)),
                  pl.BlockSpec(maps.shape, lambda i: (0, 0, 0)),
                  pl.BlockSpec((1, POOL_W), lambda i: (0, 0))],
        out_specs=pl.BlockSpec((tm, POOL_W), lambda i: (i, 0)),
        out_shape=jax.ShapeDtypeStruct((NP, POOL_W), BF16),
        compiler_params=_params(("parallel",)),
        name="pool_prompt",
    )(u, u, maps, scale)


def _pool_sample_kernel(ctx_ref, maps_ref, scale_ref, o_ref):
    for g, w in enumerate(POOL_WINDOWS):
        sl = slice(g * POOL_GROUP, (g + 1) * POOL_GROUP)
        rows = []
        for t in range(DEC_SEQ):
            r = POOL_CTX + t
            acc = ctx_ref[r, :, sl]
            for j in range(1, w):
                acc = acc + ctx_ref[r - j, :, sl]
            cnt = float(min(w, PAST_LEN - POOL_CTX + r + 1))
            rows.append(acc / cnt - ctx_ref[r, :, sl])
        d = jnp.concatenate(rows, axis=0).astype(BF16)
        y = jnp.dot(d, maps_ref[g], preferred_element_type=F32) * scale_ref[:, sl]
        bb = y.shape[0] // DEC_SEQ
        for t in range(DEC_SEQ):
            o_ref[t, :, sl] = y[t * bb:(t + 1) * bb].astype(o_ref.dtype)


def _pool_sample(ctx_t, maps, scale, bb=32):
    tc = POOL_CTX + DEC_SEQ
    return pl.pallas_call(
        _pool_sample_kernel,
        grid=(DEC_BATCH // bb,),
        in_specs=[pl.BlockSpec((tc, bb, POOL_W), lambda i: (0, i, 0)),
                  pl.BlockSpec(maps.shape, lambda i: (0, 0, 0)),
                  pl.BlockSpec((1, POOL_W), lambda i: (0, 0))],
        out_specs=pl.BlockSpec((DEC_SEQ, bb, POOL_W), lambda i: (0, i, 0)),
        out_shape=jax.ShapeDtypeStruct((DEC_SEQ, DEC_BATCH, POOL_W), BF16),
        compiler_params=_params(("parallel",)),
        name="pool_sample",
    )(ctx_t, maps, scale)


def _pad_rows(x, n):
    if x.shape[0] == n:
        return x
    return jnp.concatenate([x, jnp.zeros((n - x.shape[0], x.shape[1]), x.dtype)], axis=0)


def _mlstm_kernel(q_ref, k_ref, v_ref, og_ref, zt_ref, bias_ref, g_ref, c0_ref, n0_ref, m0_ref,
                  h_ref, c_ref, n_ref, m_ref, *, L, rows):
    @pl.when(pl.program_id(1) == 0)
    def _():
        c_ref[...] = c0_ref[...]
        n_ref[...] = n0_ref[...]
        m_ref[...] = m0_ref[...]

    zb = zt_ref[...] + bias_ref[...]
    li_all = zb[:MLSTM_HEADS]
    fx = zb[MLSTM_HEADS:]
    lf_all = jnp.minimum(fx, 0.0) - jnp.log1p(jnp.exp(-jnp.abs(fx)))
    ri = lax.broadcasted_iota(jnp.int32, (L, L), 0)
    ci = lax.broadcasted_iota(jnp.int32, (L, L), 1)
    eye = ri == ci
    tril = ci <= ri
    triu = ri <= ci

    for h in range(MLSTM_HEADS):
        hs = slice(h * MLSTM_DH, (h + 1) * MLSTM_DH)
        qc = _pad_rows(q_ref[:, hs], L)
        kc = _pad_rows(k_ref[:, hs], L) * jnp.asarray(MLSTM_DH ** -0.5, BF16)
        vc = _pad_rows(v_ref[:, hs], L)
        li_r = li_all[h:h + 1, :]
        lf_r = lf_all[h:h + 1, :]
        li_c = jnp.sum(jnp.where(eye, li_r, 0.0), axis=1, keepdims=True)
        lf_c = jnp.sum(jnp.where(eye, lf_r, 0.0), axis=1, keepdims=True)
        bcum_c = jnp.sum(jnp.where(tril, lf_r, 0.0), axis=1, keepdims=True)
        bcum_r = jnp.sum(jnp.where(triu, lf_c, 0.0), axis=0, keepdims=True)
        m_prev = m_ref[h:h + 1, :]
        c_prev = c_ref[h]
        n_prev = n_ref[h:h + 1, :]

        dmat = jnp.where(tril, bcum_c - bcum_r + li_r, NEG)
        m_t = jnp.maximum(bcum_c + m_prev, jnp.max(dmat, axis=1, keepdims=True))
        inter = jnp.exp(bcum_c + m_prev - m_t)
        wts = jnp.exp(dmat - m_t)
        s = _nt_dot(qc, kc) * wts
        num = inter * jnp.dot(qc, c_prev.astype(BF16), preferred_element_type=F32) \
            + jnp.dot(s.astype(BF16), vc, preferred_element_type=F32)
        qn = jnp.sum(qc.astype(F32) * n_prev, axis=1, keepdims=True)
        den = inter * qn + jnp.sum(s, axis=1, keepdims=True)
        hout = num / jnp.maximum(jnp.abs(den), jnp.exp(-m_t))
        hout = hout[:rows]
        ms = jnp.mean(hout * hout, axis=-1, keepdims=True)
        hn = hout * lax.rsqrt(ms + EPS) * g_ref[:, hs]
        gate = jax.nn.sigmoid(og_ref[:, hs].astype(F32))
        h_ref[:, hs] = (hn * gate).astype(h_ref.dtype)

        m_last = m_t[L - 1:L, :]
        b_last = bcum_c[L - 1:L, :]
        w_last = jnp.exp(b_last - bcum_c + li_c - m_last)
        decay = jnp.exp(b_last + m_prev - m_last)
        kw = kc.astype(F32) * w_last
        c_ref[h] = decay * c_prev + _tn_dot(kw.astype(BF16), vc)
        n_ref[h:h + 1, :] = decay * n_prev + jnp.sum(kw, axis=0, keepdims=True)
        m_ref[h:h + 1, :] = m_last


def _mlstm(mqkvo, zt, bias, gain, c0, n0, m0, *, nb, n_chunks, rows, L):
    def tok_spec(col):
        return pl.BlockSpec((rows, MLSTM_W), lambda b, c: (b * n_chunks + c, col))

    state_specs = [pl.BlockSpec((None, MLSTM_HEADS, MLSTM_DH, MLSTM_DH), lambda b, c: (b, 0, 0, 0)),
                   pl.BlockSpec((None, MLSTM_HEADS, MLSTM_DH), lambda b, c: (b, 0, 0)),
                   pl.BlockSpec((None, MLSTM_HEADS, 1), lambda b, c: (b, 0, 0))]
    return pl.pallas_call(
        functools.partial(_mlstm_kernel, L=L, rows=rows),
        grid=(nb, n_chunks),
        in_specs=[tok_spec(0), tok_spec(1), tok_spec(2), tok_spec(3),
                  pl.BlockSpec((None, 2 * MLSTM_HEADS, L), lambda b, c: (b, 0, c)),
                  pl.BlockSpec((2 * MLSTM_HEADS, 1), lambda b, c: (0, 0)),
                  pl.BlockSpec((1, MLSTM_W), lambda b, c: (0, 0))] + state_specs,
        out_specs=[pl.BlockSpec((rows, MLSTM_W), lambda b, c: (b * n_chunks + c, 0))] + state_specs,
        out_shape=[jax.ShapeDtypeStruct((nb * n_chunks * rows, MLSTM_W), BF16),
                   jax.ShapeDtypeStruct((nb, MLSTM_HEADS, MLSTM_DH, MLSTM_DH), F32),
                   jax.ShapeDtypeStruct((nb, MLSTM_HEADS, MLSTM_DH), F32),
                   jax.ShapeDtypeStruct((nb, MLSTM_HEADS, 1), F32)],
        compiler_params=_params(("parallel", "arbitrary")),
        name="mlstm",
    )(mqkvo, mqkvo, mqkvo, mqkvo, zt, bias, gain, c0, n0, m0)


def _merge_kernel(xa_ref, xb_ref, xc_ref, w_ref, ga_ref, gb_ref, gc_ref, o_ref):
    out = None
    for i, (x_ref, g_ref) in enumerate(((xa_ref, ga_ref), (xb_ref, gb_ref), (xc_ref, gc_ref))):
        br = jnp.dot(x_ref[...], w_ref[i], preferred_element_type=F32)
        term = jax.nn.sigmoid(g_ref[...]) * br
        out = term if out is None else out + term
    o_ref[...] = out.astype(o_ref.dtype)


def _merge(xa, xb, xc, w3, zg, tm=544, tn=512):
    nj = D_MODEL // tn
    x_spec = pl.BlockSpec((tm, D_MODEL), lambda i, j: (i, 0))
    return pl.pallas_call(
        _merge_kernel,
        grid=(NT // tm, nj),
        in_specs=[x_spec, x_spec, x_spec,
                  pl.BlockSpec((N_BRANCH, D_MODEL, tn), lambda i, j: (0, 0, j))]
        + [pl.BlockSpec((tm, tn), functools.partial(lambda i, j, b: (i, b * nj + j), b=b)) for b in range(N_BRANCH)],
        out_specs=pl.BlockSpec((tm, tn), lambda i, j: (i, j)),
        out_shape=jax.ShapeDtypeStruct((NT, D_MODEL), BF16),
        compiler_params=_params(("parallel", "parallel")),
        name="branch_merge",
    )(xa, xb, xc, w3, zg, zg, zg)


def _add_residual(acc, r_ref):
    return acc + r_ref[...]


def _relu2(acc):
    return jnp.square(jnp.maximum(acc, 0.0))


def _layer(x, l, rope, cache_latent, cache_krope, page_table, state_C, state_n, state_m, state_pool,
           g_mix, w_in, kv_norm_g, w_ukv, w_pool, pool_scale, b_i, b_f, mlstm_norm_g,
           w_proj_attn, w_proj_pool, w_proj_mlstm, w_out, g_mlp, w_up, w_down):
    c_tab, s1_tab, s2_tab = rope
    tm = 1088
    rope_specs = [(t, pl.BlockSpec((tm, LANE), lambda i, j, k: (i, 0))) for t in (c_tab, s1_tab, s2_tab)]

    wl = w_in[l]
    o = np.cumsum((0, Q_COLS, KV_COLS, POOL_W, 4 * MLSTM_W, 2 * MLSTM_HEADS, N_BRANCH * D_MODEL)).tolist()
    wq = wl[:, o[0]:o[1]].reshape(D_MODEL, MLA_HEADS, QK_NOPE + QK_ROPE)
    wq = jnp.pad(wq, ((0, 0), (0, 0), (0, QH - QK_NOPE - QK_ROPE))).reshape(D_MODEL, MLA_HEADS * QH).astype(BF16)
    wkv = jnp.pad(wl[:, o[1]:o[2]], ((0, 0), (0, KVP - KV_COLS))).astype(BF16)
    wu = wl[:, o[2]:o[3]].astype(BF16)
    wm = wl[:, o[3]:o[4]].astype(BF16)
    wif = jnp.pad(wl[:, o[4]:o[5]], ((0, 0), (0, LANE - 2 * MLSTM_HEADS))).astype(BF16)
    wg = wl[:, o[5]:o[6]].astype(BF16)

    w_uk = w_ukv[l][..., :QK_NOPE]
    w_uv = w_ukv[l][..., QK_NOPE:]
    eye_r = jnp.eye(QK_ROPE, dtype=F32)
    k_top = jnp.pad(w_uk, ((0, 0), (0, 0), (0, QH - QK_NOPE)))
    k_bot = jnp.broadcast_to(jnp.pad(eye_r, ((0, LANE - QK_ROPE), (QK_NOPE, QH - QK_NOPE - QK_ROPE)))[:, None, :],
                             (LANE, MLA_HEADS, QH))
    w_kb = jnp.concatenate([k_top, k_bot], axis=0).reshape(KVP, MLA_HEADS * QH)
    w_vb = jnp.pad(w_uv, ((0, LANE), (0, 0), (0, 0))).reshape(KVP, MLA_HEADS * V_HEAD)
    w_kvb = jnp.concatenate([w_kb, w_vb], axis=1).astype(BF16)
    a_top = jnp.pad(jnp.transpose(w_uk, (1, 2, 0)), ((0, 0), (0, 0), (0, KVP - KV_RANK)))
    a_bot = jnp.broadcast_to(jnp.pad(eye_r, ((0, QH - QK_NOPE - QK_ROPE), (KV_RANK, KVP - KV_RANK - QK_ROPE)))[None],
                             (MLA_HEADS, LANE, KVP))
    w_abs = jnp.concatenate([a_top, a_bot], axis=1).astype(BF16)
    w_uvh = jnp.transpose(w_uv, (1, 0, 2)).astype(BF16)

    h = _rmsnorm(x, g_mix[l], BF16)

    q = _matmul(h, wq, tm=tm, tn=2 * QH, out_dtype=BF16, name="q_proj", epilogue=_q_epilogue, extras=rope_specs)
    ckv = _matmul(h, wkv, tm=tm, tn=KVP, out_dtype=F32, name="kv_proj", epilogue=_kv_epilogue,
                  extras=[(kv_norm_g[l].reshape(1, KV_RANK), pl.BlockSpec((1, KV_RANK), lambda i, j, k: (0, 0)))]
                  + rope_specs)
    kvb = _matmul(ckv, w_kvb, rows=NP, tm=1024, tn=1024, out_dtype=BF16, name="kv_up")
    o_p = _prompt_attention(q, kvb)
    q_abs = _blockdiag_matmul(q[NP:], w_abs, BF16, "q_absorb")
    o_lat = _sample_attention(q_abs.reshape(DEC_BATCH, DEC_SEQ * MLA_HEADS, KVP),
                              ckv[NP:].reshape(DEC_BATCH, DEC_SEQ, KVP),
                              cache_latent, cache_krope, page_table, l)
    o_s = _blockdiag_matmul(o_lat.reshape(NS, MLA_HEADS * KV_RANK), w_uvh, BF16, "v_up_sample")
    o_attn = jnp.concatenate([o_p, o_s], axis=0)

    u = _matmul(h, wu, tm=tm, tn=512, out_dtype=F32, name="pool_in")
    maps = w_pool[l].astype(BF16)
    scale = pool_scale[l].reshape(1, POOL_W)
    u_s = u[NP:].reshape(DEC_BATCH, DEC_SEQ, POOL_W)
    ctx_s = jnp.concatenate([state_pool[l], u_s], axis=1)
    pm_p = _pool_prompt(u, maps, scale)
    pm_s = _pool_sample(jnp.transpose(ctx_s, (1, 0, 2)), maps, scale)
    pm = jnp.concatenate([pm_p, jnp.transpose(pm_s, (1, 0, 2)).reshape(NS, POOL_W)], axis=0)

    mz = _matmul(h, wm, tm=tm, tn=512, out_dtype=BF16, name="mlstm_in")
    zif = _matmul(h, wif, tm=tm, tn=LANE, out_dtype=F32, name="gate_in")[:, :2 * MLSTM_HEADS]
    bias = jnp.concatenate([b_i[l], b_f[l]]).reshape(2 * MLSTM_HEADS, 1)
    gain = mlstm_norm_g[l].reshape(1, MLSTM_W)
    lp, ls, rs = 256, LANE, 16
    zt_p = jnp.transpose(zif[:NP].reshape(BATCH, SEQ, 2 * MLSTM_HEADS), (0, 2, 1))
    zero_c = jnp.zeros((BATCH, MLSTM_HEADS, MLSTM_DH, MLSTM_DH), F32)
    hm_p, c_p, n_p, m_p = _mlstm(mz, zt_p, bias, gain, zero_c, zero_c[..., 0], zero_c[..., :1, 0],
                                 nb=BATCH, n_chunks=SEQ // lp, rows=lp, L=lp)
    mz_s = jnp.pad(mz[NP:].reshape(DEC_BATCH, DEC_SEQ, 4 * MLSTM_W), ((0, 0), (0, rs - DEC_SEQ), (0, 0)))
    zt_s = jnp.transpose(zif[NP:].reshape(DEC_BATCH, DEC_SEQ, 2 * MLSTM_HEADS), (0, 2, 1))
    pad_i = jnp.full((DEC_BATCH, MLSTM_HEADS, ls - DEC_SEQ), NEG, F32)
    zt_s = jnp.concatenate([zt_s, jnp.concatenate([pad_i, -pad_i], axis=1)], axis=2)
    hm_s, c_s, n_s, m_s = _mlstm(mz_s.reshape(DEC_BATCH * rs, 4 * MLSTM_W), zt_s, bias, gain,
                                 state_C[l], state_n[l], state_m[l][..., None],
                                 nb=DEC_BATCH, n_chunks=1, rows=rs, L=ls)
    hm_s = hm_s.reshape(DEC_BATCH, rs, MLSTM_W)[:, :DEC_SEQ].reshape(NS, MLSTM_W)
    hm = jnp.concatenate([hm_p[:NP], hm_s], axis=0)

    zg = _matmul(h, wg, tm=tm, tn=512, out_dtype=F32, name="branch_gates")
    w3 = jnp.stack([w_proj_attn[l], w_proj_pool[l], w_proj_mlstm[l]]).astype(BF16)
    merged = _merge(o_attn, pm, hm, w3, zg)
    res_spec = pl.BlockSpec((tm, 512), lambda i, j, k: (i, j))
    x = _matmul(merged, w_out[l].astype(BF16), tm=tm, tn=512, out_dtype=F32, name="out_proj",
                epilogue=_add_residual, extras=[(x, res_spec)])

    hf = _rmsnorm(x, g_mlp[l], BF16)
    a = _matmul(hf, w_up[l].astype(BF16), tm=tm, tn=512, out_dtype=BF16, name="mlp_up", epilogue=_relu2)
    x = _matmul(a, w_down[l].astype(BF16), tm=tm, tn=512, tk=2048, out_dtype=F32, name="mlp_down",
                epilogue=_add_residual, extras=[(x, res_spec)])

    p_state = (ckv[:NP, :KV_RANK].reshape(BATCH, SEQ, KV_RANK),
               ckv[:NP, KV_RANK:KV_COLS].reshape(BATCH, SEQ, QK_ROPE),
               c_p, n_p, m_p[..., 0], u[:NP].reshape(BATCH, SEQ, POOL_W)[:, -POOL_CTX:])
    s_state = (ckv[NP:, :KV_RANK].reshape(DEC_BATCH, DEC_SEQ, KV_RANK),
               ckv[NP:, KV_RANK:KV_COLS].reshape(DEC_BATCH, DEC_SEQ, QK_ROPE),
               c_s, n_s, m_s[..., 0], ctx_s[:, -POOL_CTX:])
    return x, p_state, s_state


def kernel(x_prompt, x_sample, cache_latent, cache_krope, page_table, state_C, state_n, state_m, state_pool,
           g_mix, w_in, kv_norm_g, w_ukv, w_pool, pool_scale, b_i, b_f, mlstm_norm_g,
           w_proj_attn, w_proj_pool, w_proj_mlstm, w_out, g_mlp, w_up, w_down, g_final):
    x = jnp.concatenate([x_prompt.reshape(NP, D_MODEL), x_sample.reshape(NS, D_MODEL)], axis=0).astype(F32)
    rope = _rope_tables()
    p_states, s_states = [], []
    for l in range(DEPTH):
        x, ps, ss = _layer(x, l, rope, cache_latent, cache_krope, page_table, state_C, state_n, state_m,
                           state_pool, g_mix, w_in, kv_norm_g, w_ukv, w_pool, pool_scale, b_i, b_f,
                           mlstm_norm_g, w_proj_attn, w_proj_pool, w_proj_mlstm, w_out, g_mlp, w_up, w_down)
        p_states.append(ps)
        s_states.append(ss)
    y = _rmsnorm(x, g_final, F32)
    y_prompt = y[:NP].reshape(BATCH, SEQ, D_MODEL).astype(x_prompt.dtype)
    y_sample = y[NP:].reshape(DEC_BATCH, DEC_SEQ, D_MODEL).astype(x_sample.dtype)
    likes = (cache_latent, cache_krope, state_C, state_n, state_m, state_pool)
    p_out = tuple(jnp.stack([s[i] for s in p_states]).astype(likes[i].dtype) for i in range(6))
    s_out = tuple(jnp.stack([s[i] for s in s_states]).astype(likes[i].dtype) for i in range(6))
    return (y_prompt, y_sample) + p_out + s_out
```

```python
import functools

import numpy as np
import jax
import jax.numpy as jnp
from jax import lax
from jax.experimental import pallas as pl
from jax.experimental.pallas import tpu as pltpu

D_MODEL = 2048
BATCH = 2
SEQ = 4096
DEPTH = 2
DEC_BATCH = 128
DEC_SEQ = 4
PAST_LEN = 8192
PAGE_SIZE = 128
N_PAGES = PAST_LEN // PAGE_SIZE

MLA_HEADS = 16
QK_NOPE = 128
QK_ROPE = 64
V_HEAD = 128
KV_RANK = 512
ROPE_THETA = 10000.0
ATTN_SCALE = (QK_NOPE + QK_ROPE) ** -0.5
Q_COLS = MLA_HEADS * (QK_NOPE + QK_ROPE)
KV_COLS = KV_RANK + QK_ROPE

POOL_WINDOWS = (2, 4, 8, 16)
POOL_W = 2048
POOL_GROUP = POOL_W // len(POOL_WINDOWS)
POOL_CTX = max(POOL_WINDOWS) - 1

MLSTM_HEADS = 8
MLSTM_DH = 256
MLSTM_W = MLSTM_HEADS * MLSTM_DH
N_BRANCH = 3
D_FF = 4 * D_MODEL
EPS = 1e-6

NP = BATCH * SEQ
NS = DEC_BATCH * DEC_SEQ
NT = NP + NS

LANE = 128
QH = 2 * LANE
KVP = KV_RANK + LANE
NEG = -1e30
VMEM_LIMIT = 56 * 1024 * 1024

F32 = jnp.float32
BF16 = jnp.bfloat16


def _params(sem, vmem=VMEM_LIMIT):
    return pltpu.CompilerParams(dimension_semantics=sem, vmem_limit_bytes=vmem)


def _nt_dot(a, b):
    return lax.dot_general(a, b, (((1,), (1,)), ((), ())), preferred_element_type=F32)


def _tn_dot(a, b):
    return lax.dot_general(a, b, (((0,), (0,)), ((), ())), preferred_element_type=F32)


def _rmsnorm_kernel(x_ref, g_ref, o_ref):
    x = x_ref[...]
    ms = jnp.mean(x * x, axis=-1, keepdims=True)
    o_ref[...] = (x * lax.rsqrt(ms + EPS) * g_ref[...]).astype(o_ref.dtype)


def _rmsnorm(x, g, out_dtype, tm=544, rows=None):
    m = rows or x.shape[0]
    d = x.shape[1]
    assert m % tm == 0, (m, tm)
    return pl.pallas_call(
        _rmsnorm_kernel,
        grid=(m // tm,),
        in_specs=[pl.BlockSpec((tm, d), lambda i: (i, 0)), pl.BlockSpec((1, d), lambda i: (0, 0))],
        out_specs=pl.BlockSpec((tm, d), lambda i: (i, 0)),
        out_shape=jax.ShapeDtypeStruct((m, d), out_dtype),
        compiler_params=_params(("parallel",)),
        name="rmsnorm",
    )(x, g.reshape(1, d))


def _matmul_kernel(*refs, n_extra, nk, epilogue):
    x_ref, w_ref = refs[0], refs[1]
    extra = refs[2:2 + n_extra]
    o_ref = refs[2 + n_extra]
    part = jnp.dot(x_ref[...].astype(BF16), w_ref[...], preferred_element_type=F32)
    if nk == 1:
        o_ref[...] = epilogue(part, *extra).astype(o_ref.dtype)
        return
    acc_ref = refs[3 + n_extra]
    k = pl.program_id(2)

    @pl.when(k == 0)
    def _():
        acc_ref[...] = part

    @pl.when(k > 0)
    def _():
        acc_ref[...] += part

    @pl.when(k == nk - 1)
    def _():
        o_ref[...] = epilogue(acc_ref[...], *extra).astype(o_ref.dtype)


def _matmul(x, w, *, tm, tn, out_dtype, name, tk=None, rows=None, epilogue=None, extras=()):
    m = rows or x.shape[0]
    kdim, n = w.shape
    tk = tk or kdim
    nk = kdim // tk
    assert m % tm == 0 and n % tn == 0 and kdim % tk == 0, (m, n, kdim, tm, tn, tk)
    if epilogue is None:
        epilogue = lambda acc: acc
    kern = functools.partial(_matmul_kernel, n_extra=len(extras), nk=nk, epilogue=epilogue)
    return pl.pallas_call(
        kern,
        grid=(m // tm, n // tn, nk),
        in_specs=[pl.BlockSpec((tm, tk), lambda i, j, k: (i, k)),
                  pl.BlockSpec((tk, tn), lambda i, j, k: (k, j))] + [s for _, s in extras],
        out_specs=pl.BlockSpec((tm, tn), lambda i, j, k: (i, j)),
        out_shape=jax.ShapeDtypeStruct((m, n), out_dtype),
        scratch_shapes=[pltpu.VMEM((tm, tn), F32)] if nk > 1 else [],
        compiler_params=_params(("parallel", "parallel", "arbitrary")),
        name=name,
    )(x, w, *[a for a, _ in extras])


def _blockdiag_kernel(x_ref, w_ref, o_ref):
    o_ref[...] = jnp.dot(x_ref[...], w_ref[...], preferred_element_type=F32).astype(o_ref.dtype)


def _blockdiag_matmul(x, w, out_dtype, name):
    m = x.shape[0]
    g, kg, ng = w.shape
    return pl.pallas_call(
        _blockdiag_kernel,
        grid=(g,),
        in_specs=[pl.BlockSpec((m, kg), lambda i: (0, i)), pl.BlockSpec((None, kg, ng), lambda i: (i, 0, 0))],
        out_specs=pl.BlockSpec((m, ng), lambda i: (0, i)),
        out_shape=jax.ShapeDtypeStruct((m, g * ng), out_dtype),
        compiler_params=_params(("parallel",)),
        name=name,
    )(x, w)


def _rope_lanes(pe, c_ref, s1_ref, s2_ref):
    return (pe * c_ref[...] + pltpu.roll(pe, LANE - QK_ROPE // 2, 1) * s1_ref[...]
            + pltpu.roll(pe, QK_ROPE // 2, 1) * s2_ref[...])


def _rope_tables():
    pos = jnp.concatenate([jnp.tile(jnp.arange(SEQ, dtype=F32), BATCH),
                           jnp.tile(PAST_LEN + jnp.arange(DEC_SEQ, dtype=F32), DEC_BATCH)])
    inv = ROPE_THETA ** (-jnp.arange(0, QK_ROPE, 2, dtype=F32) / QK_ROPE)
    ang = pos[:, None] * inv[None, :]
    cos, sin = jnp.cos(ang), jnp.sin(ang)
    z = jnp.zeros_like(cos)
    c = jnp.concatenate([cos, cos, z, z], axis=1)
    s1 = jnp.concatenate([-sin, z, z, z], axis=1)
    s2 = jnp.concatenate([z, sin, z, z], axis=1)
    return c, s1, s2


def _q_epilogue(acc, c_ref, s1_ref, s2_ref):
    parts = []
    for h in range(acc.shape[1] // QH):
        parts.append(acc[:, h * QH:h * QH + LANE])
        parts.append(_rope_lanes(acc[:, h * QH + LANE:(h + 1) * QH], c_ref, s1_ref, s2_ref))
    return jnp.concatenate(parts, axis=1) * ATTN_SCALE


def _kv_epilogue(acc, g_ref, c_ref, s1_ref, s2_ref):
    lat = acc[:, :KV_RANK]
    ms = jnp.mean(lat * lat, axis=-1, keepdims=True)
    lat = lat * lax.rsqrt(ms + EPS) * g_ref[...]
    return jnp.concatenate([lat, _rope_lanes(acc[:, KV_RANK:], c_ref, s1_ref, s2_ref)], axis=1)


def _vt_kernel(w_ref, x_ref, o_ref):
    o_ref[...] = _nt_dot(w_ref[...], x_ref[...].astype(BF16)).astype(o_ref.dtype)


def _vt_matmul(w_t, x, *, rows, tt, tn=1024):
    n, kdim = w_t.shape
    return pl.pallas_call(
        _vt_kernel,
        grid=(rows // tt, n // tn),
        in_specs=[pl.BlockSpec((tn, kdim), lambda j, i: (i, 0)), pl.BlockSpec((tt, kdim), lambda j, i: (j, 0))],
        out_specs=pl.BlockSpec((None, tn, tt), lambda j, i: (j, i, 0)),
        out_shape=jax.ShapeDtypeStruct((rows // tt, n, tt), BF16),
        compiler_params=_params(("parallel", "parallel")),
        name="v_up_t",
    )(w_t, x)


def _attn_kernel(q_ref, k_ref, vt_ref, o_ref, m_ref, l_ref, acc_ref, *, tq):
    iq = pl.program_id(2)
    q = q_ref[...]
    m_ref[...] = jnp.full_like(m_ref, NEG)
    l_ref[...] = jnp.zeros_like(l_ref)
    acc_ref[...] = jnp.zeros_like(acc_ref)

    def step(ik, diagonal):
        start = pl.multiple_of(ik * tq, tq)
        s = _nt_dot(k_ref[pl.ds(start, tq), :], q)
        if diagonal:
            ki = lax.broadcasted_iota(jnp.int32, s.shape, 0)
            qi = lax.broadcasted_iota(jnp.int32, s.shape, 1)
            s = jnp.where(ki <= qi, s, NEG)
        m_prev = m_ref[...]
        m_new = jnp.maximum(m_prev, jnp.max(s, axis=0, keepdims=True))
        alpha = jnp.exp(m_prev - m_new)
        p = jnp.exp(s - m_new)
        l_ref[...] = alpha * l_ref[...] + jnp.sum(p, axis=0, keepdims=True)
        acc_ref[...] = alpha * acc_ref[...] + jnp.dot(vt_ref[ik], p.astype(BF16), preferred_element_type=F32)
        m_ref[...] = m_new

    def body(ik, carry):
        step(ik, False)
        return carry

    lax.fori_loop(0, iq, body, 0)
    step(iq, True)
    o_ref[...] = jnp.transpose(acc_ref[...] / l_ref[...]).astype(o_ref.dtype)


def _prompt_attention(q, k, vt, *, batch, seq, heads, tq, out_rows):
    nq = seq // tq
    return pl.pallas_call(
        functools.partial(_attn_kernel, tq=tq),
        grid=(batch, heads, nq),
        in_specs=[pl.BlockSpec((tq, QH), lambda b, h, i: (b * nq + i, h)),
                  pl.BlockSpec((seq, QH), lambda b, h, i: (b, h)),
                  pl.BlockSpec((nq, V_HEAD, tq), lambda b, h, i: (b, h, 0))],
        out_specs=pl.BlockSpec((tq, V_HEAD), lambda b, h, i: (b * nq + i, h)),
        out_shape=jax.ShapeDtypeStruct((out_rows, heads * V_HEAD), BF16),
        scratch_shapes=[pltpu.VMEM((1, tq), F32), pltpu.VMEM((1, tq), F32), pltpu.VMEM((V_HEAD, tq), F32)],
        compiler_params=_params(("parallel", "parallel", "arbitrary")),
        name="prompt_attention",
    )(q, k, vt)


def _decode_kernel(pt_ref, q_ref, new_ref, lat_hbm, kr_hbm, o_ref, lat_buf, kr_buf, sem, m_ref, l_ref, acc_ref,
                   *, pg, n_groups, n_chains, layer):
    g = pl.program_id(1)
    step = pl.program_id(0) * n_groups + g
    slot = lax.rem(step, 2)

    def page_copies(step_idx, slot_idx):
        cps = []
        for i in range(pg):
            page = pt_ref[step_idx * pg + i]
            cps.append(pltpu.make_async_copy(lat_hbm.at[layer, page], lat_buf.at[slot_idx, i], sem.at[0, slot_idx]))
            cps.append(pltpu.make_async_copy(kr_hbm.at[layer, page], kr_buf.at[slot_idx, i], sem.at[1, slot_idx]))
        return cps

    @pl.when(step == 0)
    def _():
        for cp in page_copies(0, 0):
            cp.start()

    @pl.when(step + 1 < DEC_BATCH * n_groups)
    def _():
        for cp in page_copies(step + 1, 1 - slot):
            cp.start()

    @pl.when(g == 0)
    def _():
        m_ref[...] = jnp.full_like(m_ref, NEG)
        l_ref[...] = jnp.zeros_like(l_ref)
        acc_ref[...] = jnp.zeros_like(acc_ref)

    for cp in page_copies(step, slot):
        cp.wait()

    q = q_ref[...]
    q_lat, q_pe = q[:, :KV_RANK], q[:, KV_RANK:]
    per = pg // n_chains
    state = [(m_ref[c], l_ref[c], acc_ref[c]) for c in range(n_chains)]
    for c in range(n_chains):
        pages = range(c * per, (c + 1) * per)
        lat = jnp.concatenate([lat_buf[slot, i].astype(BF16) for i in pages], axis=0)
        krt = jnp.concatenate([kr_buf[slot, i] for i in pages], axis=1)
        krt = jnp.concatenate([krt, jnp.zeros_like(krt)], axis=0).astype(BF16)
        s = _nt_dot(q_lat, lat) + jnp.dot(q_pe, krt, preferred_element_type=F32)
        m_prev, l_prev, acc_prev = state[c]
        m_new = jnp.maximum(m_prev, jnp.max(s, axis=-1, keepdims=True))
        alpha = jnp.exp(m_prev - m_new)
        p = jnp.exp(s - m_new)
        state[c] = (m_new, alpha * l_prev + jnp.sum(p, axis=-1, keepdims=True),
                    alpha * acc_prev + jnp.dot(p.astype(BF16), lat, preferred_element_type=F32))
    for c in range(n_chains):
        m_ref[c], l_ref[c], acc_ref[c] = state[c]

    @pl.when(g == n_groups - 1)
    def _():
        m_run = m_ref[0]
        for c in range(1, n_chains):
            m_run = jnp.maximum(m_run, m_ref[c])
        l_run = jnp.zeros_like(m_run)
        acc = jnp.zeros(acc_ref.shape[1:], F32)
        for c in range(n_chains):
            w = jnp.exp(m_ref[c] - m_run)
            l_run = l_run + w * l_ref[c]
            acc = acc + w * acc_ref[c]
        qf = q.astype(F32)
        row = lax.broadcasted_iota(jnp.int32, m_run.shape, 0)
        for t in range(DEC_SEQ):
            kn = new_ref[t:t + 1, :]
            st = jnp.sum(qf * kn, axis=-1, keepdims=True)
            st = jnp.where(row >= t * MLA_HEADS, st, NEG)
            m_nxt = jnp.maximum(m_run, st)
            a = jnp.exp(m_run - m_nxt)
            pt = jnp.exp(st - m_nxt)
            l_run = a * l_run + pt
            acc = a * acc + pt * kn[:, :KV_RANK]
            m_run = m_nxt
        o_ref[...] = (acc / l_run).astype(o_ref.dtype)


def _sample_attention(q_abs, ckv_new, cache_latent, cache_krope_t, page_table, layer, pg=16, n_chains=2):
    n_groups = N_PAGES // pg
    rows = DEC_SEQ * MLA_HEADS
    grid_spec = pltpu.PrefetchScalarGridSpec(
        num_scalar_prefetch=1,
        grid=(DEC_BATCH, n_groups),
        in_specs=[pl.BlockSpec((None, rows, KVP), lambda b, g, pt: (b, 0, 0)),
                  pl.BlockSpec((None, DEC_SEQ, KVP), lambda b, g, pt: (b, 0, 0)),
                  pl.BlockSpec(memory_space=pl.ANY),
                  pl.BlockSpec(memory_space=pl.ANY)],
        out_specs=pl.BlockSpec((None, rows, KV_RANK), lambda b, g, pt: (b, 0, 0)),
        scratch_shapes=[pltpu.VMEM((2, pg, PAGE_SIZE, KV_RANK), F32), pltpu.VMEM((2, pg, QK_ROPE, PAGE_SIZE), F32),
                        pltpu.SemaphoreType.DMA((2, 2)),
                        pltpu.VMEM((n_chains, rows, 1), F32), pltpu.VMEM((n_chains, rows, 1), F32),
                        pltpu.VMEM((n_chains, rows, KV_RANK), F32)],
    )
    return pl.pallas_call(
        functools.partial(_decode_kernel, pg=pg, n_groups=n_groups, n_chains=n_chains, layer=layer),
        grid_spec=grid_spec,
        out_shape=jax.ShapeDtypeStruct((DEC_BATCH, rows, KV_RANK), BF16),
        compiler_params=_params(("arbitrary", "arbitrary")),
        name="sample_attention",
    )(page_table.reshape(-1), q_abs, ckv_new, cache_latent, cache_krope_t)


def _pool_prompt_kernel(u_ref, halo_ref, maps_ref, scale_ref, o_ref, *, tm):
    row0 = lax.rem(pl.program_id(0) * tm, SEQ)
    pos1 = row0 + 1 + lax.broadcasted_iota(jnp.int32, (tm, 1), 0)
    for g, w in enumerate(POOL_WINDOWS):
        sl = slice(g * POOL_GROUP, (g + 1) * POOL_GROUP)
        u = u_ref[:, sl]
        halo = jnp.where(row0 != 0, halo_ref[:, sl], 0.0)
        acc = jnp.concatenate([halo, u], axis=0)
        shift = 1
        while shift < w:
            acc = acc + pltpu.roll(acc, shift, 0)
            shift *= 2
        cnt = jnp.minimum(w, pos1).astype(F32)
        d = acc[POOL_CTX + 1:, :] / cnt - u
        y = jnp.dot(d.astype(BF16), maps_ref[g], preferred_element_type=F32) * scale_ref[:, sl]
        o_ref[:, sl] = y.astype(o_ref.dtype)


def _pool_prompt(u, maps, scale, tm=512):
    hb = POOL_CTX + 1
    return pl.pallas_call(
        functools.partial(_pool_prompt_kernel, tm=tm),
        grid=(NP // tm,),
        in_specs=[pl.BlockSpec((tm, POOL_W), lambda i: (i, 0)),
                  pl.BlockSpec((hb, POOL_W), lambda i: (jnp.maximum(i * (tm // hb) - 1, 0), 0)),
                  pl.BlockSpec(maps.shape, lambda i: (0, 0, 0)),
                  pl.BlockSpec((1, POOL_W), lambda i: (0, 0))],
        out_specs=pl.BlockSpec((tm, POOL_W), lambda i: (i, 0)),
        out_shape=jax.ShapeDtypeStruct((NT, POOL_W), BF16),
        compiler_params=_params(("parallel",)),
        name="pool_prompt",
    )(u, u, maps, scale)


def _pool_sample_kernel(ctx_ref, maps_ref, scale_ref, o_ref):
    for g, w in enumerate(POOL_WINDOWS):
        sl = slice(g * POOL_GROUP, (g + 1) * POOL_GROUP)
        rows = []
        for t in range(DEC_SEQ):
            r = POOL_CTX + t
            acc = ctx_ref[r, :, sl]
            for j in range(1, w):
                acc = acc + ctx_ref[r - j, :, sl]
            cnt = float(min(w, PAST_LEN - POOL_CTX + r + 1))
            rows.append(acc / cnt - ctx_ref[r, :, sl])
        d = jnp.concatenate(rows, axis=0).astype(BF16)
        y = jnp.dot(d, maps_ref[g], preferred_element_type=F32) * scale_ref[:, sl]
        bb = y.shape[0] // DEC_SEQ
        for t in range(DEC_SEQ):
            o_ref[t, :, sl] = y[t * bb:(t + 1) * bb].astype(o_ref.dtype)


def _pool_sample(ctx_t, maps, scale, bb=32):
    tc = POOL_CTX + DEC_SEQ
    return pl.pallas_call(
        _pool_sample_kernel,
        grid=(DEC_BATCH // bb,),
        in_specs=[pl.BlockSpec((tc, bb, POOL_W), lambda i: (0, i, 0)),
                  pl.BlockSpec(maps.shape, lambda i: (0, 0, 0)),
                  pl.BlockSpec((1, POOL_W), lambda i: (0, 0))],
        out_specs=pl.BlockSpec((DEC_SEQ, bb, POOL_W), lambda i: (0, i, 0)),
        out_shape=jax.ShapeDtypeStruct((DEC_SEQ, DEC_BATCH, POOL_W), BF16),
        compiler_params=_params(("parallel",)),
        name="pool_sample",
    )(ctx_t, maps, scale)


def _pad_rows(x, n):
    if x.shape[0] == n:
        return x
    return jnp.concatenate([x, jnp.zeros((n - x.shape[0], x.shape[1]), x.dtype)], axis=0)


def _mlstm_kernel(q_ref, k_ref, v_ref, og_ref, zt_ref, bias_ref, g_ref, c0_ref, n0_ref, m0_ref, *refs, L, rows):
    h_ref, c_ref, n_ref, m_ref = refs[-4:]

    @pl.when(pl.program_id(1) == 0)
    def _():
        c_ref[...] = c0_ref[...]
        n_ref[...] = n0_ref[...]
        m_ref[...] = m0_ref[...]

    zb = zt_ref[...] + bias_ref[...]
    li_all = zb[:MLSTM_HEADS]
    fx = zb[MLSTM_HEADS:]
    lf_all = jnp.minimum(fx, 0.0) - jnp.log1p(jnp.exp(-jnp.abs(fx)))
    ri = lax.broadcasted_iota(jnp.int32, (L, L), 0)
    ci = lax.broadcasted_iota(jnp.int32, (L, L), 1)
    eye = ri == ci
    tril = ci <= ri
    triu = ri <= ci

    for h in range(MLSTM_HEADS):
        hs = slice(h * MLSTM_DH, (h + 1) * MLSTM_DH)
        qc = _pad_rows(q_ref[:, hs], L)
        kc = _pad_rows(k_ref[:, hs], L) * jnp.asarray(MLSTM_DH ** -0.5, BF16)
        vc = _pad_rows(v_ref[:, hs], L)
        li_r = li_all[h:h + 1, :]
        lf_r = lf_all[h:h + 1, :]
        li_c = jnp.sum(jnp.where(eye, li_r, 0.0), axis=1, keepdims=True)
        lf_c = jnp.sum(jnp.where(eye, lf_r, 0.0), axis=1, keepdims=True)
        bcum_c = jnp.sum(jnp.where(tril, lf_r, 0.0), axis=1, keepdims=True)
        bcum_r = jnp.sum(jnp.where(triu, lf_c, 0.0), axis=0, keepdims=True)
        m_prev = m_ref[h:h + 1, :]
        c_prev = c_ref[h]
        n_prev = n_ref[h:h + 1, :]

        dmat = jnp.where(tril, bcum_c - bcum_r + li_r, NEG)
        m_t = jnp.maximum(bcum_c + m_prev, jnp.max(dmat, axis=1, keepdims=True))
        inter = jnp.exp(bcum_c + m_prev - m_t)
        wts = jnp.exp(dmat - m_t)
        s = _nt_dot(qc, kc) * wts
        num = inter * jnp.dot(qc, c_prev.astype(BF16), preferred_element_type=F32) \
            + jnp.dot(s.astype(BF16), vc, preferred_element_type=F32)
        qn = jnp.sum(qc.astype(F32) * n_prev, axis=1, keepdims=True)
        den = inter * qn + jnp.sum(s, axis=1, keepdims=True)
        hout = num / jnp.maximum(jnp.abs(den), jnp.exp(-m_t))
        hout = hout[:rows]
        ms = jnp.mean(hout * hout, axis=-1, keepdims=True)
        hn = hout * lax.rsqrt(ms + EPS) * g_ref[:, hs]
        gate = jax.nn.sigmoid(og_ref[:, hs].astype(F32))
        h_ref[:, hs] = (hn * gate).astype(h_ref.dtype)

        m_last = m_t[L - 1:L, :]
        b_last = bcum_c[L - 1:L, :]
        w_last = jnp.exp(b_last - bcum_c + li_c - m_last)
        decay = jnp.exp(b_last + m_prev - m_last)
        kw = kc.astype(F32) * w_last
        c_ref[h] = decay * c_prev + _tn_dot(kw.astype(BF16), vc)
        n_ref[h:h + 1, :] = decay * n_prev + jnp.sum(kw, axis=0, keepdims=True)
        m_ref[h:h + 1, :] = m_last


def _mlstm(mqkvo, zt, bias, gain, init, init_layer, prev, layer, *, nb, n_chunks, rows, L, out_rows):
    def tok_spec(col):
        return pl.BlockSpec((rows, MLSTM_W), lambda b, c: (b * n_chunks + c, col))

    def state_specs(li):
        return [pl.BlockSpec((None, None, MLSTM_HEADS, MLSTM_DH, MLSTM_DH), lambda b, c: (li, b, 0, 0, 0)),
                pl.BlockSpec((None, None, MLSTM_HEADS, MLSTM_DH), lambda b, c: (li, b, 0, 0)),
                pl.BlockSpec((None, None, MLSTM_HEADS, 1), lambda b, c: (li, b, 0, 0))]

    prev = tuple(prev) if prev is not None else ()
    n_in = 10
    return pl.pallas_call(
        functools.partial(_mlstm_kernel, L=L, rows=rows),
        grid=(nb, n_chunks),
        in_specs=[tok_spec(0), tok_spec(1), tok_spec(2), tok_spec(3),
                  pl.BlockSpec((None, 2 * MLSTM_HEADS, L), lambda b, c: (b, 0, c)),
                  pl.BlockSpec((2 * MLSTM_HEADS, 1), lambda b, c: (0, 0)),
                  pl.BlockSpec((1, MLSTM_W), lambda b, c: (0, 0))] + state_specs(init_layer)
        + [pl.BlockSpec(memory_space=pl.ANY)] * len(prev),
        out_specs=[pl.BlockSpec((rows, MLSTM_W), lambda b, c: (b * n_chunks + c, 0))] + state_specs(layer),
        out_shape=[jax.ShapeDtypeStruct((out_rows, MLSTM_W), BF16),
                   jax.ShapeDtypeStruct((DEPTH, nb, MLSTM_HEADS, MLSTM_DH, MLSTM_DH), F32),
                   jax.ShapeDtypeStruct((DEPTH, nb, MLSTM_HEADS, MLSTM_DH), F32),
                   jax.ShapeDtypeStruct((DEPTH, nb, MLSTM_HEADS, 1), F32)],
        input_output_aliases={n_in + i: 1 + i for i in range(len(prev))},
        compiler_params=_params(("parallel", "arbitrary")),
        name="mlstm",
    )(mqkvo, mqkvo, mqkvo, mqkvo, zt, bias, gain, *init, *prev)


def _merge_kernel(xa_ref, xb_ref, xc_ref, w_ref, ga_ref, gb_ref, gc_ref, o_ref):
    out = None
    for i, (x_ref, g_ref) in enumerate(((xa_ref, ga_ref), (xb_ref, gb_ref), (xc_ref, gc_ref))):
        br = jnp.dot(x_ref[...], w_ref[i], preferred_element_type=F32)
        term = jax.nn.sigmoid(g_ref[...]) * br
        out = term if out is None else out + term
    o_ref[...] = out.astype(o_ref.dtype)


def _merge(xa, xb, xc, w3, zg, tm=544, tn=512):
    nj = D_MODEL // tn
    x_spec = pl.BlockSpec((tm, D_MODEL), lambda i, j: (i, 0))
    return pl.pallas_call(
        _merge_kernel,
        grid=(NT // tm, nj),
        in_specs=[x_spec, x_spec, x_spec,
                  pl.BlockSpec((N_BRANCH, D_MODEL, tn), lambda i, j: (0, 0, j))]
        + [pl.BlockSpec((tm, tn), functools.partial(lambda i, j, b: (i, b * nj + j), b=b)) for b in range(N_BRANCH)],
        out_specs=pl.BlockSpec((tm, tn), lambda i, j: (i, j)),
        out_shape=jax.ShapeDtypeStruct((NT, D_MODEL), BF16),
        compiler_params=_params(("parallel", "parallel")),
        name="branch_merge",
    )(xa, xb, xc, w3, zg, zg, zg)


def _add_residual(acc, r_ref):
    return acc + r_ref[...]


def _relu2(acc):
    return jnp.square(jnp.maximum(acc, 0.0))


def _layer(x, l, rope, mstate, cache_latent, cache_krope_t, page_table, state_C, state_n, state_m, state_pool,
           g_mix, w_in, kv_norm_g, w_ukv, w_pool, pool_scale, b_i, b_f, mlstm_norm_g,
           w_proj_attn, w_proj_pool, w_proj_mlstm, w_out, g_mlp, w_up, w_down):
    c_tab, s1_tab, s2_tab = rope
    tm = 1088
    rope_specs = [(t, pl.BlockSpec((tm, LANE), lambda i, j, k: (i, 0))) for t in (c_tab, s1_tab, s2_tab)]

    wl = w_in[l]
    o = np.cumsum((0, Q_COLS, KV_COLS, POOL_W, 4 * MLSTM_W, 2 * MLSTM_HEADS, N_BRANCH * D_MODEL)).tolist()
    wq = wl[:, o[0]:o[1]].reshape(D_MODEL, MLA_HEADS, QK_NOPE + QK_ROPE)
    wq = jnp.pad(wq, ((0, 0), (0, 0), (0, QH - QK_NOPE - QK_ROPE))).reshape(D_MODEL, MLA_HEADS * QH).astype(BF16)
    wkv = jnp.pad(wl[:, o[1]:o[2]], ((0, 0), (0, KVP - KV_COLS))).astype(BF16)
    wu = wl[:, o[2]:o[3]].astype(BF16)
    wm = wl[:, o[3]:o[4]].astype(BF16)
    wif = jnp.pad(wl[:, o[4]:o[5]], ((0, 0), (0, LANE - 2 * MLSTM_HEADS))).astype(BF16)
    wg = wl[:, o[5]:o[6]].astype(BF16)

    w_uk = w_ukv[l][..., :QK_NOPE]
    w_uv = w_ukv[l][..., QK_NOPE:]
    eye_r = jnp.eye(QK_ROPE, dtype=F32)
    k_top = jnp.pad(w_uk, ((0, 0), (0, 0), (0, QH - QK_NOPE)))
    k_bot = jnp.broadcast_to(jnp.pad(eye_r, ((0, LANE - QK_ROPE), (QK_NOPE, QH - QK_NOPE - QK_ROPE)))[:, None, :],
                             (LANE, MLA_HEADS, QH))
    w_kb = jnp.concatenate([k_top, k_bot], axis=0).reshape(KVP, MLA_HEADS * QH).astype(BF16)
    w_vt = jnp.pad(jnp.transpose(w_uv, (1, 2, 0)).reshape(MLA_HEADS * V_HEAD, KV_RANK),
                   ((0, 0), (0, KVP - KV_RANK))).astype(BF16)
    a_top = jnp.pad(jnp.transpose(w_uk, (1, 2, 0)), ((0, 0), (0, 0), (0, KVP - KV_RANK)))
    a_bot = jnp.broadcast_to(jnp.pad(eye_r, ((0, QH - QK_NOPE - QK_ROPE), (KV_RANK, KVP - KV_RANK - QK_ROPE)))[None],
                             (MLA_HEADS, LANE, KVP))
    w_abs = jnp.concatenate([a_top, a_bot], axis=1).astype(BF16)
    w_uvh = jnp.transpose(w_uv, (1, 0, 2)).astype(BF16)

    h = _rmsnorm(x, g_mix[l], BF16)

    q = _matmul(h, wq, tm=tm, tn=2 * QH, out_dtype=BF16, name="q_proj", epilogue=_q_epilogue, extras=rope_specs)
    ckv = _matmul(h, wkv, tm=tm, tn=KVP, out_dtype=F32, name="kv_proj", epilogue=_kv_epilogue,
                  extras=[(kv_norm_g[l].reshape(1, KV_RANK), pl.BlockSpec((1, KV_RANK), lambda i, j, k: (0, 0)))]
                  + rope_specs)
    tq = 512
    k_p = _matmul(ckv, w_kb, rows=NP, tm=1024, tn=1024, out_dtype=BF16, name="k_up")
    vt_p = _vt_matmul(w_vt, ckv, rows=NP, tt=tq)
    o_p = _prompt_attention(q, k_p, vt_p, batch=BATCH, seq=SEQ, heads=MLA_HEADS, tq=tq, out_rows=NT)
    q_abs = _blockdiag_matmul(q[NP:], w_abs, BF16, "q_absorb")
    o_lat = _sample_attention(q_abs.reshape(DEC_BATCH, DEC_SEQ * MLA_HEADS, KVP),
                              ckv[NP:].reshape(DEC_BATCH, DEC_SEQ, KVP),
                              cache_latent, cache_krope_t, page_table, l)
    o_s = _blockdiag_matmul(o_lat.reshape(NS, MLA_HEADS * KV_RANK), w_uvh, BF16, "v_up_sample")
    o_attn = o_p.at[NP:].set(o_s)

    u = _matmul(h, wu, tm=tm, tn=512, out_dtype=F32, name="pool_in")
    maps = w_pool[l].astype(BF16)
    scale = pool_scale[l].reshape(1, POOL_W)
    u_s = u[NP:].reshape(DEC_BATCH, DEC_SEQ, POOL_W)
    ctx_s = jnp.concatenate([state_pool[l], u_s], axis=1)
    pm_s = _pool_sample(jnp.transpose(ctx_s, (1, 0, 2)), maps, scale)
    pm = _pool_prompt(u, maps, scale).at[NP:].set(jnp.transpose(pm_s, (1, 0, 2)).reshape(NS, POOL_W))

    mz = _matmul(h, wm, tm=tm, tn=1024, out_dtype=BF16, name="mlstm_in")
    zif = _matmul(h, wif, tm=tm, tn=LANE, out_dtype=F32, name="gate_in")[:, :2 * MLSTM_HEADS]
    bias = jnp.concatenate([b_i[l], b_f[l]]).reshape(2 * MLSTM_HEADS, 1)
    gain = mlstm_norm_g[l].reshape(1, MLSTM_W)
    lp, ls, rs = 256, LANE, 16
    zt_p = jnp.transpose(zif[:NP].reshape(BATCH, SEQ, 2 * MLSTM_HEADS), (0, 2, 1))
    zero_c = jnp.zeros((1, BATCH, MLSTM_HEADS, MLSTM_DH, MLSTM_DH), F32)
    p_prev, s_prev = mstate if mstate is not None else (None, None)
    hm_p, *p_new = _mlstm(mz, zt_p, bias, gain, (zero_c, zero_c[..., 0], zero_c[..., :1, 0]), 0, p_prev, l,
                          nb=BATCH, n_chunks=SEQ // lp, rows=lp, L=lp, out_rows=NT)
    mz_s = jnp.pad(mz[NP:].reshape(DEC_BATCH, DEC_SEQ, 4 * MLSTM_W), ((0, 0), (0, rs - DEC_SEQ), (0, 0)))
    zt_s = jnp.transpose(zif[NP:].reshape(DEC_BATCH, DEC_SEQ, 2 * MLSTM_HEADS), (0, 2, 1))
    pad_i = jnp.full((DEC_BATCH, MLSTM_HEADS, ls - DEC_SEQ), NEG, F32)
    zt_s = jnp.concatenate([zt_s, jnp.concatenate([pad_i, -pad_i], axis=1)], axis=2)
    hm_s, *s_new = _mlstm(mz_s.reshape(DEC_BATCH * rs, 4 * MLSTM_W), zt_s, bias, gain,
                          (state_C, state_n, state_m[..., None]), l, s_prev, l,
                          nb=DEC_BATCH, n_chunks=1, rows=rs, L=ls, out_rows=DEC_BATCH * rs)
    hm_s = hm_s.reshape(DEC_BATCH, rs, MLSTM_W)[:, :DEC_SEQ].reshape(NS, MLSTM_W)
    hm = hm_p.at[NP:].set(hm_s)

    zg = _matmul(h, wg, tm=2 * tm, tn=512, out_dtype=F32, name="branch_gates")
    w3 = jnp.stack([w_proj_attn[l], w_proj_pool[l], w_proj_mlstm[l]]).astype(BF16)
    merged = _merge(o_attn, pm, hm, w3, zg)
    res_spec = pl.BlockSpec((tm, 512), lambda i, j, k: (i, j))
    x = _matmul(merged, w_out[l].astype(BF16), tm=tm, tn=512, out_dtype=F32, name="out_proj",
                epilogue=_add_residual, extras=[(x, res_spec)])

    hf = _rmsnorm(x, g_mlp[l], BF16)
    a = _matmul(hf, w_up[l].astype(BF16), tm=tm, tn=512, out_dtype=BF16, name="mlp_up", epilogue=_relu2)
    x = _matmul(a, w_down[l].astype(BF16), tm=tm // 2, tn=512, out_dtype=F32, name="mlp_down",
                epilogue=_add_residual, extras=[(x, pl.BlockSpec((tm // 2, 512), lambda i, j, k: (i, j)))])

    p_state = (ckv[:NP, :KV_RANK].reshape(BATCH, SEQ, KV_RANK),
               ckv[:NP, KV_RANK:KV_COLS].reshape(BATCH, SEQ, QK_ROPE),
               jnp.stack([u[(b + 1) * SEQ - POOL_CTX:(b + 1) * SEQ] for b in range(BATCH)]))
    s_state = (ckv[NP:, :KV_RANK].reshape(DEC_BATCH, DEC_SEQ, KV_RANK),
               ckv[NP:, KV_RANK:KV_COLS].reshape(DEC_BATCH, DEC_SEQ, QK_ROPE),
               ctx_s[:, -POOL_CTX:])
    return x, p_state, s_state, (p_new, s_new)


def kernel(x_prompt, x_sample, cache_latent, cache_krope, page_table, state_C, state_n, state_m, state_pool,
           g_mix, w_in, kv_norm_g, w_ukv, w_pool, pool_scale, b_i, b_f, mlstm_norm_g,
           w_proj_attn, w_proj_pool, w_proj_mlstm, w_out, g_mlp, w_up, w_down, g_final):
    x = jnp.concatenate([x_prompt.reshape(NP, D_MODEL), x_sample.reshape(NS, D_MODEL)], axis=0).astype(F32)
    rope = _rope_tables()
    cache_krope_t = jnp.swapaxes(cache_krope, 2, 3)
    p_states, s_states, mstate = [], [], None
    for l in range(DEPTH):
        x, ps, ss, mstate = _layer(x, l, rope, mstate, cache_latent, cache_krope_t, page_table, state_C, state_n,
                                   state_m, state_pool, g_mix, w_in, kv_norm_g, w_ukv, w_pool, pool_scale, b_i, b_f,
                                   mlstm_norm_g, w_proj_attn, w_proj_pool, w_proj_mlstm, w_out, g_mlp, w_up, w_down)
        p_states.append(ps)
        s_states.append(ss)
    y_prompt = _rmsnorm(x, g_final, F32, tm=512, rows=NP).reshape(BATCH, SEQ, D_MODEL).astype(x_prompt.dtype)
    y_sample = _rmsnorm(x[NP:], g_final, F32, tm=NS).reshape(DEC_BATCH, DEC_SEQ, D_MODEL).astype(x_sample.dtype)

    def stack(states, i, like):
        return jnp.stack([s[i] for s in states]).astype(like.dtype)

    (p_c, p_n, p_m), (s_c, s_n, s_m) = mstate
    return (y_prompt, y_sample,
            stack(p_states, 0, cache_latent), stack(p_states, 1, cache_krope),
            p_c.astype(state_C.dtype), p_n.astype(state_n.dtype), p_m[..., 0].astype(state_m.dtype),
            stack(p_states, 2, state_pool),
            stack(s_states, 0, cache_latent), stack(s_states, 1, cache_krope),
            s_c.astype(state_C.dtype), s_n.astype(state_n.dtype), s_m[..., 0].astype(state_m.dtype),
            stack(s_states, 2, state_pool))
```

```python
import functools

import numpy as np
import jax
import jax.numpy as jnp
from jax import lax
from jax.experimental import pallas as pl
from jax.experimental.pallas import tpu as pltpu

D_MODEL = 2048
BATCH = 2
SEQ = 4096
DEPTH = 2
DEC_BATCH = 128
DEC_SEQ = 4
PAST_LEN = 8192
PAGE_SIZE = 128
N_PAGES = PAST_LEN // PAGE_SIZE

MLA_HEADS = 16
QK_NOPE = 128
QK_ROPE = 64
V_HEAD = 128
KV_RANK = 512
ROPE_THETA = 10000.0
ATTN_SCALE = (QK_NOPE + QK_ROPE) ** -0.5
Q_COLS = MLA_HEADS * (QK_NOPE + QK_ROPE)
KV_COLS = KV_RANK + QK_ROPE

POOL_WINDOWS = (2, 4, 8, 16)
POOL_W = 2048
POOL_GROUP = POOL_W // len(POOL_WINDOWS)
POOL_CTX = max(POOL_WINDOWS) - 1

MLSTM_HEADS = 8
MLSTM_DH = 256
MLSTM_W = MLSTM_HEADS * MLSTM_DH
N_BRANCH = 3
D_FF = 4 * D_MODEL
EPS = 1e-6

NP = BATCH * SEQ
NS = DEC_BATCH * DEC_SEQ
NT = NP + NS

LANE = 128
QH = 2 * LANE
KVP = KV_RANK + LANE
NEG = -1e30
VMEM_LIMIT = 56 * 1024 * 1024

F32 = jnp.float32
BF16 = jnp.bfloat16


def _params(sem, vmem=VMEM_LIMIT):
    return pltpu.CompilerParams(dimension_semantics=sem, vmem_limit_bytes=vmem)


def _nt_dot(a, b):
    return lax.dot_general(a, b, (((1,), (1,)), ((), ())), preferred_element_type=F32)


def _tn_dot(a, b):
    return lax.dot_general(a, b, (((0,), (0,)), ((), ())), preferred_element_type=F32)


def _rmsnorm_kernel(x_ref, g_ref, o_ref):
    x = x_ref[...]
    ms = jnp.mean(x * x, axis=-1, keepdims=True)
    o_ref[...] = (x * lax.rsqrt(ms + EPS) * g_ref[...]).astype(o_ref.dtype)


def _rmsnorm(x, g, out_dtype, tm=544, rows=None):
    m = rows or x.shape[0]
    d = x.shape[1]
    assert m % tm == 0, (m, tm)
    return pl.pallas_call(
        _rmsnorm_kernel,
        grid=(m // tm,),
        in_specs=[pl.BlockSpec((tm, d), lambda i: (i, 0)), pl.BlockSpec((1, d), lambda i: (0, 0))],
        out_specs=pl.BlockSpec((tm, d), lambda i: (i, 0)),
        out_shape=jax.ShapeDtypeStruct((m, d), out_dtype),
        compiler_params=_params(("parallel",)),
        name="rmsnorm",
    )(x, g.reshape(1, d))


def _matmul_kernel(*refs, n_extra, nk, epilogue):
    x_ref, w_ref = refs[0], refs[1]
    extra = refs[2:2 + n_extra]
    o_ref = refs[2 + n_extra]
    part = jnp.dot(x_ref[...].astype(BF16), w_ref[...], preferred_element_type=F32)
    if nk == 1:
        o_ref[...] = epilogue(part, *extra).astype(o_ref.dtype)
        return
    acc_ref = refs[3 + n_extra]
    k = pl.program_id(2)

    @pl.when(k == 0)
    def _():
        acc_ref[...] = part

    @pl.when(k > 0)
    def _():
        acc_ref[...] += part

    @pl.when(k == nk - 1)
    def _():
        o_ref[...] = epilogue(acc_ref[...], *extra).astype(o_ref.dtype)


def _matmul(x, w, *, tm, tn, out_dtype, name, tk=None, rows=None, layer=None, epilogue=None, extras=()):
    m = rows or x.shape[0]
    kdim, n = w.shape[-2:]
    tk = tk or kdim
    nk = kdim // tk
    assert m % tm == 0 and n % tn == 0 and kdim % tk == 0, (m, n, kdim, tm, tn, tk)
    if epilogue is None:
        epilogue = lambda acc: acc
    if layer is None:
        w_spec = pl.BlockSpec((tk, tn), lambda i, j, k: (k, j))
    else:
        w_spec = pl.BlockSpec((None, tk, tn), lambda i, j, k: (layer, k, j))
    kern = functools.partial(_matmul_kernel, n_extra=len(extras), nk=nk, epilogue=epilogue)
    return pl.pallas_call(
        kern,
        grid=(m // tm, n // tn, nk),
        in_specs=[pl.BlockSpec((tm, tk), lambda i, j, k: (i, k)), w_spec] + [s for _, s in extras],
        out_specs=pl.BlockSpec((tm, tn), lambda i, j, k: (i, j)),
        out_shape=jax.ShapeDtypeStruct((m, n), out_dtype),
        scratch_shapes=[pltpu.VMEM((tm, tn), F32)] if nk > 1 else [],
        compiler_params=_params(("parallel", "parallel", "arbitrary")),
        name=name,
    )(x, w, *[a for a, _ in extras])


def _blockdiag_kernel(x_ref, w_ref, o_ref):
    o_ref[...] = jnp.dot(x_ref[...], w_ref[...], preferred_element_type=F32).astype(o_ref.dtype)


def _blockdiag_matmul(x, w, out_dtype, name):
    m = x.shape[0]
    g, kg, ng = w.shape
    return pl.pallas_call(
        _blockdiag_kernel,
        grid=(g,),
        in_specs=[pl.BlockSpec((m, kg), lambda i: (0, i)), pl.BlockSpec((None, kg, ng), lambda i: (i, 0, 0))],
        out_specs=pl.BlockSpec((m, ng), lambda i: (0, i)),
        out_shape=jax.ShapeDtypeStruct((m, g * ng), out_dtype),
        compiler_params=_params(("parallel",)),
        name=name,
    )(x, w)


def _rope_lanes(pe, c_ref, s1_ref, s2_ref):
    return (pe * c_ref[...] + pltpu.roll(pe, LANE - QK_ROPE // 2, 1) * s1_ref[...]
            + pltpu.roll(pe, QK_ROPE // 2, 1) * s2_ref[...])


def _rope_tables():
    pos = jnp.concatenate([jnp.tile(jnp.arange(SEQ, dtype=F32), BATCH),
                           jnp.tile(PAST_LEN + jnp.arange(DEC_SEQ, dtype=F32), DEC_BATCH)])
    inv = ROPE_THETA ** (-jnp.arange(0, QK_ROPE, 2, dtype=F32) / QK_ROPE)
    ang = pos[:, None] * inv[None, :]
    cos, sin = jnp.cos(ang), jnp.sin(ang)
    z = jnp.zeros_like(cos)
    c = jnp.concatenate([cos, cos, z, z], axis=1)
    s1 = jnp.concatenate([-sin, z, z, z], axis=1)
    s2 = jnp.concatenate([z, sin, z, z], axis=1)
    return c, s1, s2


def _q_epilogue(acc, c_ref, s1_ref, s2_ref):
    parts = []
    for h in range(acc.shape[1] // QH):
        parts.append(acc[:, h * QH:h * QH + LANE])
        parts.append(_rope_lanes(acc[:, h * QH + LANE:(h + 1) * QH], c_ref, s1_ref, s2_ref))
    return jnp.concatenate(parts, axis=1) * ATTN_SCALE


def _kv_epilogue(acc, g_ref, c_ref, s1_ref, s2_ref):
    lat = acc[:, :KV_RANK]
    ms = jnp.mean(lat * lat, axis=-1, keepdims=True)
    lat = lat * lax.rsqrt(ms + EPS) * g_ref[...]
    return jnp.concatenate([lat, _rope_lanes(acc[:, KV_RANK:], c_ref, s1_ref, s2_ref)], axis=1)


def _vt_kernel(w_ref, x_ref, o_ref):
    o_ref[...] = _nt_dot(w_ref[...], x_ref[...].astype(BF16)).astype(o_ref.dtype)


def _vt_matmul(w_t, x, *, rows, tt, tn=1024):
    n, kdim = w_t.shape
    return pl.pallas_call(
        _vt_kernel,
        grid=(rows // tt, n // tn),
        in_specs=[pl.BlockSpec((tn, kdim), lambda j, i: (i, 0)), pl.BlockSpec((tt, kdim), lambda j, i: (j, 0))],
        out_specs=pl.BlockSpec((None, tn, tt), lambda j, i: (j, i, 0)),
        out_shape=jax.ShapeDtypeStruct((rows // tt, n, tt), BF16),
        compiler_params=_params(("parallel", "parallel")),
        name="v_up_t",
    )(w_t, x)


def _attn_kernel(q_ref, k_ref, vt_ref, o_ref, s0_ref, s1_ref, m_ref, l_ref, acc_ref, *, tq):
    iq = pl.program_id(2)
    q = q_ref[...]
    m_ref[...] = jnp.full_like(m_ref, NEG)
    l_ref[...] = jnp.zeros_like(l_ref)
    acc_ref[...] = jnp.zeros_like(acc_ref)

    def scores(ik, s_ref):
        start = pl.multiple_of(ik * tq, tq)
        s_ref[...] = _nt_dot(k_ref[pl.ds(start, tq), :], q)

    def update(ik, s_ref, diagonal):
        s = s_ref[...]
        if diagonal:
            ki = lax.broadcasted_iota(jnp.int32, s.shape, 0)
            qi = lax.broadcasted_iota(jnp.int32, s.shape, 1)
            s = jnp.where(ki <= qi, s, NEG)
        m_prev = m_ref[...]
        m_new = jnp.maximum(m_prev, jnp.max(s, axis=0, keepdims=True))
        alpha = jnp.exp(m_prev - m_new)
        p = jnp.exp(s - m_new)
        l_ref[...] = alpha * l_ref[...] + jnp.sum(p, axis=0, keepdims=True)
        acc_ref[...] = alpha * acc_ref[...] + jnp.dot(vt_ref[ik], p.astype(BF16), preferred_element_type=F32)
        m_ref[...] = m_new

    scores(0, s0_ref)

    def body(j, carry):
        ik = 2 * j
        scores(ik + 1, s1_ref)
        update(ik, s0_ref, False)
        scores(ik + 2, s0_ref)
        update(ik + 1, s1_ref, False)
        return carry

    lax.fori_loop(0, lax.div(iq, 2), body, 0)

    @pl.when(lax.rem(iq, 2) == 0)
    def _():
        update(iq, s0_ref, True)

    @pl.when(lax.rem(iq, 2) == 1)
    def _():
        scores(iq, s1_ref)
        update(iq - 1, s0_ref, False)
        update(iq, s1_ref, True)

    o_ref[...] = jnp.transpose(acc_ref[...] / l_ref[...]).astype(o_ref.dtype)


def _prompt_attention(q, k, vt, *, batch, seq, heads, tq, out_rows):
    nq = seq // tq
    return pl.pallas_call(
        functools.partial(_attn_kernel, tq=tq),
        grid=(batch, heads, nq),
        in_specs=[pl.BlockSpec((tq, QH), lambda b, h, i: (b * nq + i, h)),
                  pl.BlockSpec((seq, QH), lambda b, h, i: (b, h)),
                  pl.BlockSpec((nq, V_HEAD, tq), lambda b, h, i: (b, h, 0))],
        out_specs=pl.BlockSpec((tq, V_HEAD), lambda b, h, i: (b * nq + i, h)),
        out_shape=jax.ShapeDtypeStruct((out_rows, heads * V_HEAD), BF16),
        scratch_shapes=[pltpu.VMEM((tq, tq), F32), pltpu.VMEM((tq, tq), F32),
                        pltpu.VMEM((1, tq), F32), pltpu.VMEM((1, tq), F32), pltpu.VMEM((V_HEAD, tq), F32)],
        compiler_params=_params(("parallel", "parallel", "arbitrary")),
        name="prompt_attention",
    )(q, k, vt)


def _decode_kernel(pt_ref, q_ref, new_ref, lat_hbm, kr_hbm, o_ref, lat_buf, kr_buf, sem, m_ref, l_ref, acc_ref,
                   *, pg, n_groups, n_chains, layer):
    g = pl.program_id(1)
    step = pl.program_id(0) * n_groups + g
    slot = lax.rem(step, 2)

    def page_copies(step_idx, slot_idx):
        cps = []
        for i in range(pg):
            page = pt_ref[step_idx * pg + i]
            cps.append(pltpu.make_async_copy(lat_hbm.at[layer, page], lat_buf.at[slot_idx, i], sem.at[0, slot_idx]))
            cps.append(pltpu.make_async_copy(kr_hbm.at[layer, page], kr_buf.at[slot_idx, i], sem.at[1, slot_idx]))
        return cps

    @pl.when(step == 0)
    def _():
        for cp in page_copies(0, 0):
            cp.start()

    @pl.when(step + 1 < DEC_BATCH * n_groups)
    def _():
        for cp in page_copies(step + 1, 1 - slot):
            cp.start()

    @pl.when(g == 0)
    def _():
        m_ref[...] = jnp.full_like(m_ref, NEG)
        l_ref[...] = jnp.zeros_like(l_ref)
        acc_ref[...] = jnp.zeros_like(acc_ref)

    for cp in page_copies(step, slot):
        cp.wait()

    q = q_ref[...]
    q_lat, q_pe = q[:, :KV_RANK], q[:, KV_RANK:]
    per = pg // n_chains
    lats, scores = [], []
    for c in range(n_chains):
        pages = range(c * per, (c + 1) * per)
        lat = jnp.concatenate([lat_buf[slot, i].astype(BF16) for i in pages], axis=0)
        krt = jnp.concatenate([kr_buf[slot, i] for i in pages], axis=1)
        krt = jnp.concatenate([krt, jnp.zeros_like(krt)], axis=0).astype(BF16)
        lats.append(lat)
        scores.append(_nt_dot(q_lat, lat) + jnp.dot(q_pe, krt, preferred_element_type=F32))
    for c in range(n_chains):
        s, m_prev = scores[c], m_ref[c]
        m_new = jnp.maximum(m_prev, jnp.max(s, axis=-1, keepdims=True))
        alpha = jnp.exp(m_prev - m_new)
        p = jnp.exp(s - m_new)
        l_ref[c] = alpha * l_ref[c] + jnp.sum(p, axis=-1, keepdims=True)
        acc_ref[c] = alpha * acc_ref[c] + jnp.dot(p.astype(BF16), lats[c], preferred_element_type=F32)
        m_ref[c] = m_new

    @pl.when(g == n_groups - 1)
    def _():
        m_run = m_ref[0]
        for c in range(1, n_chains):
            m_run = jnp.maximum(m_run, m_ref[c])
        l_run = jnp.zeros_like(m_run)
        acc = jnp.zeros(acc_ref.shape[1:], F32)
        for c in range(n_chains):
            w = jnp.exp(m_ref[c] - m_run)
            l_run = l_run + w * l_ref[c]
            acc = acc + w * acc_ref[c]
        qf = q.astype(F32)
        row = lax.broadcasted_iota(jnp.int32, m_run.shape, 0)
        for t in range(DEC_SEQ):
            kn = new_ref[t:t + 1, :]
            st = jnp.sum(qf * kn, axis=-1, keepdims=True)
            st = jnp.where(row >= t * MLA_HEADS, st, NEG)
            m_nxt = jnp.maximum(m_run, st)
            a = jnp.exp(m_run - m_nxt)
            pt = jnp.exp(st - m_nxt)
            l_run = a * l_run + pt
            acc = a * acc + pt * kn[:, :KV_RANK]
            m_run = m_nxt
        o_ref[...] = (acc / l_run).astype(o_ref.dtype)


def _sample_attention(q_abs, ckv_new, cache_latent, cache_krope_t, page_table, layer, pg=16, n_chains=4):
    n_groups = N_PAGES // pg
    rows = DEC_SEQ * MLA_HEADS
    grid_spec = pltpu.PrefetchScalarGridSpec(
        num_scalar_prefetch=1,
        grid=(DEC_BATCH, n_groups),
        in_specs=[pl.BlockSpec((None, rows, KVP), lambda b, g, pt: (b, 0, 0)),
                  pl.BlockSpec((None, DEC_SEQ, KVP), lambda b, g, pt: (b, 0, 0)),
                  pl.BlockSpec(memory_space=pl.ANY),
                  pl.BlockSpec(memory_space=pl.ANY)],
        out_specs=pl.BlockSpec((None, rows, KV_RANK), lambda b, g, pt: (b, 0, 0)),
        scratch_shapes=[pltpu.VMEM((2, pg, PAGE_SIZE, KV_RANK), F32), pltpu.VMEM((2, pg, QK_ROPE, PAGE_SIZE), F32),
                        pltpu.SemaphoreType.DMA((2, 2)),
                        pltpu.VMEM((n_chains, rows, 1), F32), pltpu.VMEM((n_chains, rows, 1), F32),
                        pltpu.VMEM((n_chains, rows, KV_RANK), F32)],
    )
    return pl.pallas_call(
        functools.partial(_decode_kernel, pg=pg, n_groups=n_groups, n_chains=n_chains, layer=layer),
        grid_spec=grid_spec,
        out_shape=jax.ShapeDtypeStruct((DEC_BATCH, rows, KV_RANK), BF16),
        compiler_params=_params(("arbitrary", "arbitrary")),
        name="sample_attention",
    )(page_table.reshape(-1), q_abs, ckv_new, cache_latent, cache_krope_t)


def _pool_prompt_kernel(u_ref, halo_ref, maps_ref, scale_ref, o_ref, *, tm):
    row0 = lax.rem(pl.program_id(0) * tm, SEQ)
    pos1 = row0 + 1 + lax.broadcasted_iota(jnp.int32, (tm, 1), 0)
    for g, w in enumerate(POOL_WINDOWS):
        sl = slice(g * POOL_GROUP, (g + 1) * POOL_GROUP)
        u = u_ref[:, sl]
        halo = jnp.where(row0 != 0, halo_ref[:, sl], 0.0)
        acc = jnp.concatenate([halo, u], axis=0)
        shift = 1
        while shift < w:
            acc = acc + pltpu.roll(acc, shift, 0)
            shift *= 2
        cnt = jnp.minimum(w, pos1).astype(F32)
        d = acc[POOL_CTX + 1:, :] / cnt - u
        y = jnp.dot(d.astype(BF16), maps_ref[g], preferred_element_type=F32) * scale_ref[:, sl]
        o_ref[:, sl] = y.astype(o_ref.dtype)


def _pool_prompt(u, maps, scale, tm=512):
    hb = POOL_CTX + 1
    return pl.pallas_call(
        functools.partial(_pool_prompt_kernel, tm=tm),
        grid=(NP // tm,),
        in_specs=[pl.BlockSpec((tm, POOL_W), lambda i: (i, 0)),
                  pl.BlockSpec((hb, POOL_W), lambda i: (jnp.maximum(i * (tm // hb) - 1, 0), 0)),
                  pl.BlockSpec(maps.shape, lambda i: (0, 0, 0)),
                  pl.BlockSpec((1, POOL_W), lambda i: (0, 0))],
        out_specs=pl.BlockSpec((tm, POOL_W), lambda i: (i, 0)),
        out_shape=jax.ShapeDtypeStruct((NT, POOL_W), BF16),
        compiler_params=_params(("parallel",)),
        name="pool_prompt",
    )(u, u, maps, scale)


def _pool_sample_kernel(ctx_ref, maps_ref, scale_ref, o_ref):
    for g, w in enumerate(POOL_WINDOWS):
        sl = slice(g * POOL_GROUP, (g + 1) * POOL_GROUP)
        rows = []
        for t in range(DEC_SEQ):
            r = POOL_CTX + t
            acc = ctx_ref[r, :, sl]
            for j in range(1, w):
                acc = acc + ctx_ref[r - j, :, sl]
            cnt = float(min(w, PAST_LEN - POOL_CTX + r + 1))
            rows.append(acc / cnt - ctx_ref[r, :, sl])
        d = jnp.concatenate(rows, axis=0).astype(BF16)
        y = jnp.dot(d, maps_ref[g], preferred_element_type=F32) * scale_ref[:, sl]
        bb = y.shape[0] // DEC_SEQ
        for t in range(DEC_SEQ):
            o_ref[t, :, sl] = y[t * bb:(t + 1) * bb].astype(o_ref.dtype)


def _pool_sample(ctx_t, maps, scale, bb=32):
    tc = POOL_CTX + DEC_SEQ
    return pl.pallas_call(
        _pool_sample_kernel,
        grid=(DEC_BATCH // bb,),
        in_specs=[pl.BlockSpec((tc, bb, POOL_W), lambda i: (0, i, 0)),
                  pl.BlockSpec(maps.shape, lambda i: (0, 0, 0)),
                  pl.BlockSpec((1, POOL_W), lambda i: (0, 0))],
        out_specs=pl.BlockSpec((DEC_SEQ, bb, POOL_W), lambda i: (0, i, 0)),
        out_shape=jax.ShapeDtypeStruct((DEC_SEQ, DEC_BATCH, POOL_W), BF16),
        compiler_params=_params(("parallel",)),
        name="pool_sample",
    )(ctx_t, maps, scale)


def _pad_rows(x, n):
    if x.shape[0] == n:
        return x
    return jnp.concatenate([x, jnp.zeros((n - x.shape[0], x.shape[1]), x.dtype)], axis=0)


def _mlstm_kernel(q_ref, k_ref, v_ref, og_ref, zt_ref, bias_ref, g_ref, c0_ref, n0_ref, m0_ref, *refs, L, rows):
    h_ref, c_ref, n_ref, m_ref = refs[-4:]

    @pl.when(pl.program_id(1) == 0)
    def _():
        c_ref[...] = c0_ref[...]
        n_ref[...] = n0_ref[...]
        m_ref[...] = m0_ref[...]

    zb = zt_ref[...] + bias_ref[...]
    li_all = zb[:MLSTM_HEADS]
    fx = zb[MLSTM_HEADS:]
    lf_all = jnp.minimum(fx, 0.0) - jnp.log1p(jnp.exp(-jnp.abs(fx)))
    ri = lax.broadcasted_iota(jnp.int32, (L, L), 0)
    ci = lax.broadcasted_iota(jnp.int32, (L, L), 1)
    eye = ri == ci
    tril = ci <= ri
    triu = ri <= ci

    for h in range(MLSTM_HEADS):
        hs = slice(h * MLSTM_DH, (h + 1) * MLSTM_DH)
        qc = _pad_rows(q_ref[:, hs], L)
        kc = _pad_rows(k_ref[:, hs], L) * jnp.asarray(MLSTM_DH ** -0.5, BF16)
        vc = _pad_rows(v_ref[:, hs], L)
        li_r = li_all[h:h + 1, :]
        lf_r = lf_all[h:h + 1, :]
        li_c = jnp.sum(jnp.where(eye, li_r, 0.0), axis=1, keepdims=True)
        lf_c = jnp.sum(jnp.where(eye, lf_r, 0.0), axis=1, keepdims=True)
        bcum_c = jnp.sum(jnp.where(tril, lf_r, 0.0), axis=1, keepdims=True)
        bcum_r = jnp.sum(jnp.where(triu, lf_c, 0.0), axis=0, keepdims=True)
        m_prev = m_ref[h:h + 1, :]
        c_prev = c_ref[h]
        n_prev = n_ref[h:h + 1, :]

        dmat = jnp.where(tril, bcum_c - bcum_r + li_r, NEG)
        m_t = jnp.maximum(bcum_c + m_prev, jnp.max(dmat, axis=1, keepdims=True))
        inter = jnp.exp(bcum_c + m_prev - m_t)
        wts = jnp.exp(dmat - m_t)
        s = _nt_dot(qc, kc) * wts
        num = inter * jnp.dot(qc, c_prev.astype(BF16), preferred_element_type=F32) \
            + jnp.dot(s.astype(BF16), vc, preferred_element_type=F32)
        qn = jnp.sum(qc.astype(F32) * n_prev, axis=1, keepdims=True)
        den = inter * qn + jnp.sum(s, axis=1, keepdims=True)
        hout = num / jnp.maximum(jnp.abs(den), jnp.exp(-m_t))
        hout = hout[:rows]
        ms = jnp.mean(hout * hout, axis=-1, keepdims=True)
        hn = hout * lax.rsqrt(ms + EPS) * g_ref[:, hs]
        gate = jax.nn.sigmoid(og_ref[:, hs].astype(F32))
        h_ref[:, hs] = (hn * gate).astype(h_ref.dtype)

        m_last = m_t[L - 1:L, :]
        b_last = bcum_c[L - 1:L, :]
        w_last = jnp.exp(b_last - bcum_c + li_c - m_last)
        decay = jnp.exp(b_last + m_prev - m_last)
        kw = kc.astype(F32) * w_last
        c_ref[h] = decay * c_prev + _tn_dot(kw.astype(BF16), vc)
        n_ref[h:h + 1, :] = decay * n_prev + jnp.sum(kw, axis=0, keepdims=True)
        m_ref[h:h + 1, :] = m_last


def _mlstm(mqkvo, zt, bias, gain, init, init_layer, prev, layer, *, nb, n_chunks, rows, L, out_rows):
    def tok_spec(col):
        return pl.BlockSpec((rows, MLSTM_W), lambda b, c: (b * n_chunks + c, col))

    def state_specs(li):
        return [pl.BlockSpec((None, None, MLSTM_HEADS, MLSTM_DH, MLSTM_DH), lambda b, c: (li, b, 0, 0, 0)),
                pl.BlockSpec((None, None, MLSTM_HEADS, MLSTM_DH), lambda b, c: (li, b, 0, 0)),
                pl.BlockSpec((None, None, MLSTM_HEADS, 1), lambda b, c: (li, b, 0, 0))]

    prev = tuple(prev) if prev is not None else ()
    n_in = 10
    return pl.pallas_call(
        functools.partial(_mlstm_kernel, L=L, rows=rows),
        grid=(nb, n_chunks),
        in_specs=[tok_spec(0), tok_spec(1), tok_spec(2), tok_spec(3),
                  pl.BlockSpec((None, 2 * MLSTM_HEADS, L), lambda b, c: (b, 0, c)),
                  pl.BlockSpec((2 * MLSTM_HEADS, 1), lambda b, c: (0, 0)),
                  pl.BlockSpec((1, MLSTM_W), lambda b, c: (0, 0))] + state_specs(init_layer)
        + [pl.BlockSpec(memory_space=pl.ANY)] * len(prev),
        out_specs=[pl.BlockSpec((rows, MLSTM_W), lambda b, c: (b * n_chunks + c, 0))] + state_specs(layer),
        out_shape=[jax.ShapeDtypeStruct((out_rows, MLSTM_W), BF16),
                   jax.ShapeDtypeStruct((DEPTH, nb, MLSTM_HEADS, MLSTM_DH, MLSTM_DH), F32),
                   jax.ShapeDtypeStruct((DEPTH, nb, MLSTM_HEADS, MLSTM_DH), F32),
                   jax.ShapeDtypeStruct((DEPTH, nb, MLSTM_HEADS, 1), F32)],
        input_output_aliases={n_in + i: 1 + i for i in range(len(prev))},
        compiler_params=_params(("parallel", "arbitrary")),
        name="mlstm",
    )(mqkvo, mqkvo, mqkvo, mqkvo, zt, bias, gain, *init, *prev)


def _merge_kernel(xa_ref, xb_ref, xc_ref, wa_ref, wb_ref, wc_ref, ga_ref, gb_ref, gc_ref, o_ref):
    out = None
    for x_ref, w_ref, g_ref in ((xa_ref, wa_ref, ga_ref), (xb_ref, wb_ref, gb_ref), (xc_ref, wc_ref, gc_ref)):
        br = jnp.dot(x_ref[...], w_ref[...], preferred_element_type=F32)
        term = jax.nn.sigmoid(g_ref[...]) * br
        out = term if out is None else out + term
    o_ref[...] = out.astype(o_ref.dtype)


def _merge(xs, ws, zg, layer, tm=544, tn=512):
    nj = D_MODEL // tn
    assert NT % tm == 0 and D_MODEL % tn == 0
    x_spec = pl.BlockSpec((tm, D_MODEL), lambda i, j: (i, 0))
    w_spec = pl.BlockSpec((None, D_MODEL, tn), lambda i, j: (layer, 0, j))
    return pl.pallas_call(
        _merge_kernel,
        grid=(NT // tm, nj),
        in_specs=[x_spec] * N_BRANCH + [w_spec] * N_BRANCH
        + [pl.BlockSpec((tm, tn), functools.partial(lambda i, j, b: (i, b * nj + j), b=b)) for b in range(N_BRANCH)],
        out_specs=pl.BlockSpec((tm, tn), lambda i, j: (i, j)),
        out_shape=jax.ShapeDtypeStruct((NT, D_MODEL), BF16),
        compiler_params=_params(("parallel", "parallel")),
        name="branch_merge",
    )(*xs, *ws, zg, zg, zg)


def _add_residual(acc, r_ref):
    return acc + r_ref[...]


def _relu2(acc):
    return jnp.square(jnp.maximum(acc, 0.0))


def _layer(x, l, rope, mstate, cache_latent, cache_krope_t, page_table, state_C, state_n, state_m, state_pool,
           g_mix, w_in, kv_norm_g, w_ukv, w_pool, pool_scale, b_i, b_f, mlstm_norm_g,
           w_proj_attn, w_proj_pool, w_proj_mlstm, w_out, g_mlp, w_up, w_down):
    c_tab, s1_tab, s2_tab = rope
    tm = 1088
    rope_specs = [(t, pl.BlockSpec((tm, LANE), lambda i, j, k: (i, 0))) for t in (c_tab, s1_tab, s2_tab)]

    wl = w_in[l]
    o = np.cumsum((0, Q_COLS, KV_COLS, POOL_W, 4 * MLSTM_W, 2 * MLSTM_HEADS, N_BRANCH * D_MODEL)).tolist()
    wq = wl[:, o[0]:o[1]].reshape(D_MODEL, MLA_HEADS, QK_NOPE + QK_ROPE)
    wq = jnp.pad(wq, ((0, 0), (0, 0), (0, QH - QK_NOPE - QK_ROPE))).reshape(D_MODEL, MLA_HEADS * QH).astype(BF16)
    wkv = jnp.pad(wl[:, o[1]:o[2]], ((0, 0), (0, KVP - KV_COLS))).astype(BF16)
    wu = wl[:, o[2]:o[3]].astype(BF16)
    wm = wl[:, o[3]:o[4]].astype(BF16)
    wif = jnp.pad(wl[:, o[4]:o[5]], ((0, 0), (0, LANE - 2 * MLSTM_HEADS))).astype(BF16)
    wg = wl[:, o[5]:o[6]].astype(BF16)

    w_uk = w_ukv[l][..., :QK_NOPE]
    w_uv = w_ukv[l][..., QK_NOPE:]
    eye_r = jnp.eye(QK_ROPE, dtype=F32)
    k_top = jnp.pad(w_uk, ((0, 0), (0, 0), (0, QH - QK_NOPE)))
    k_bot = jnp.broadcast_to(jnp.pad(eye_r, ((0, LANE - QK_ROPE), (QK_NOPE, QH - QK_NOPE - QK_ROPE)))[:, None, :],
                             (LANE, MLA_HEADS, QH))
    w_kb = jnp.concatenate([k_top, k_bot], axis=0).reshape(KVP, MLA_HEADS * QH).astype(BF16)
    w_vt = jnp.pad(jnp.transpose(w_uv, (1, 2, 0)).reshape(MLA_HEADS * V_HEAD, KV_RANK),
                   ((0, 0), (0, KVP - KV_RANK))).astype(BF16)
    a_top = jnp.pad(jnp.transpose(w_uk, (1, 2, 0)), ((0, 0), (0, 0), (0, KVP - KV_RANK)))
    a_bot = jnp.broadcast_to(jnp.pad(eye_r, ((0, QH - QK_NOPE - QK_ROPE), (KV_RANK, KVP - KV_RANK - QK_ROPE)))[None],
                             (MLA_HEADS, LANE, KVP))
    w_abs = jnp.concatenate([a_top, a_bot], axis=1).astype(BF16)
    w_uvh = jnp.transpose(w_uv, (1, 0, 2)).astype(BF16)

    h = _rmsnorm(x, g_mix[l], BF16)

    q = _matmul(h, wq, tm=tm, tn=4 * QH, out_dtype=BF16, name="q_proj", epilogue=_q_epilogue, extras=rope_specs)
    ckv = _matmul(h, wkv, tm=tm, tn=KVP, out_dtype=F32, name="kv_proj", epilogue=_kv_epilogue,
                  extras=[(kv_norm_g[l].reshape(1, KV_RANK), pl.BlockSpec((1, KV_RANK), lambda i, j, k: (0, 0)))]
                  + rope_specs)
    tq = 512
    k_p = _matmul(ckv, w_kb, rows=NP, tm=1024, tn=1024, out_dtype=BF16, name="k_up")
    vt_p = _vt_matmul(w_vt, ckv, rows=NP, tt=tq)
    o_p = _prompt_attention(q, k_p, vt_p, batch=BATCH, seq=SEQ, heads=MLA_HEADS, tq=tq, out_rows=NT)
    q_abs = _blockdiag_matmul(q[NP:], w_abs, BF16, "q_absorb")
    o_lat = _sample_attention(q_abs.reshape(DEC_BATCH, DEC_SEQ * MLA_HEADS, KVP),
                              ckv[NP:].reshape(DEC_BATCH, DEC_SEQ, KVP),
                              cache_latent, cache_krope_t, page_table, l)
    o_s = _blockdiag_matmul(o_lat.reshape(NS, MLA_HEADS * KV_RANK), w_uvh, BF16, "v_up_sample")
    o_attn = o_p.at[NP:].set(o_s)

    u = _matmul(h, wu, tm=2 * tm, tn=512, out_dtype=F32, name="pool_in")
    maps = w_pool[l].astype(BF16)
    scale = pool_scale[l].reshape(1, POOL_W)
    u_s = u[NP:].reshape(DEC_BATCH, DEC_SEQ, POOL_W)
    ctx_s = jnp.concatenate([state_pool[l], u_s], axis=1)
    pm_s = _pool_sample(jnp.transpose(ctx_s, (1, 0, 2)), maps, scale)
    pm = _pool_prompt(u, maps, scale).at[NP:].set(jnp.transpose(pm_s, (1, 0, 2)).reshape(NS, POOL_W))

    mz = _matmul(h, wm, tm=2 * tm, tn=1024, out_dtype=BF16, name="mlstm_in")
    zif = _matmul(h, wif, tm=tm, tn=LANE, out_dtype=F32, name="gate_in")[:, :2 * MLSTM_HEADS]
    bias = jnp.concatenate([b_i[l], b_f[l]]).reshape(2 * MLSTM_HEADS, 1)
    gain = mlstm_norm_g[l].reshape(1, MLSTM_W)
    lp, ls, rs = 256, LANE, 16
    zt_p = jnp.transpose(zif[:NP].reshape(BATCH, SEQ, 2 * MLSTM_HEADS), (0, 2, 1))
    zero_c = jnp.zeros((1, BATCH, MLSTM_HEADS, MLSTM_DH, MLSTM_DH), F32)
    p_prev, s_prev = mstate if mstate is not None else (None, None)
    hm_p, *p_new = _mlstm(mz, zt_p, bias, gain, (zero_c, zero_c[..., 0], zero_c[..., :1, 0]), 0, p_prev, l,
                          nb=BATCH, n_chunks=SEQ // lp, rows=lp, L=lp, out_rows=NT)
    mz_s = jnp.pad(mz[NP:].reshape(DEC_BATCH, DEC_SEQ, 4 * MLSTM_W), ((0, 0), (0, rs - DEC_SEQ), (0, 0)))
    zt_s = jnp.transpose(zif[NP:].reshape(DEC_BATCH, DEC_SEQ, 2 * MLSTM_HEADS), (0, 2, 1))
    pad_i = jnp.full((DEC_BATCH, MLSTM_HEADS, ls - DEC_SEQ), NEG, F32)
    zt_s = jnp.concatenate([zt_s, jnp.concatenate([pad_i, -pad_i], axis=1)], axis=2)
    hm_s, *s_new = _mlstm(mz_s.reshape(DEC_BATCH * rs, 4 * MLSTM_W), zt_s, bias, gain,
                          (state_C, state_n, state_m[..., None]), l, s_prev, l,
                          nb=DEC_BATCH, n_chunks=1, rows=rs, L=ls, out_rows=DEC_BATCH * rs)
    hm_s = hm_s.reshape(DEC_BATCH, rs, MLSTM_W)[:, :DEC_SEQ].reshape(NS, MLSTM_W)
    hm = hm_p.at[NP:].set(hm_s)

    zg = _matmul(h, wg, tm=2 * tm, tn=512, out_dtype=F32, name="branch_gates")
    merged = _merge((o_attn, pm, hm), [w.astype(BF16) for w in (w_proj_attn, w_proj_pool, w_proj_mlstm)], zg, l)
    x = _matmul(merged, w_out.astype(BF16), layer=l, tm=tm, tn=1024, out_dtype=F32, name="out_proj",
                epilogue=_add_residual, extras=[(x, pl.BlockSpec((tm, 1024), lambda i, j, k: (i, j)))])

    hf = _rmsnorm(x, g_mlp[l], BF16)
    a = _matmul(hf, w_up.astype(BF16), layer=l, tm=2 * tm, tn=1024, out_dtype=BF16, name="mlp_up", epilogue=_relu2)
    x = _matmul(a, w_down.astype(BF16), layer=l, tm=tm // 2, tn=512, out_dtype=F32, name="mlp_down",
                epilogue=_add_residual, extras=[(x, pl.BlockSpec((tm // 2, 512), lambda i, j, k: (i, j)))])

    p_state = (ckv[:NP, :KV_RANK].reshape(BATCH, SEQ, KV_RANK),
               ckv[:NP, KV_RANK:KV_COLS].reshape(BATCH, SEQ, QK_ROPE),
               jnp.stack([u[(b + 1) * SEQ - POOL_CTX:(b + 1) * SEQ] for b in range(BATCH)]))
    s_state = (ckv[NP:, :KV_RANK].reshape(DEC_BATCH, DEC_SEQ, KV_RANK),
               ckv[NP:, KV_RANK:KV_COLS].reshape(DEC_BATCH, DEC_SEQ, QK_ROPE),
               ctx_s[:, -POOL_CTX:])
    return x, p_state, s_state, (p_new, s_new)


def kernel(x_prompt, x_sample, cache_latent, cache_krope, page_table, state_C, state_n, state_m, state_pool,
           g_mix, w_in, kv_norm_g, w_ukv, w_pool, pool_scale, b_i, b_f, mlstm_norm_g,
           w_proj_attn, w_proj_pool, w_proj_mlstm, w_out, g_mlp, w_up, w_down, g_final):
    x = jnp.concatenate([x_prompt.reshape(NP, D_MODEL), x_sample.reshape(NS, D_MODEL)], axis=0).astype(F32)
    rope = _rope_tables()
    cache_krope_t = jnp.swapaxes(cache_krope, 2, 3)
    p_states, s_states, mstate = [], [], None
    for l in range(DEPTH):
        x, ps, ss, mstate = _layer(x, l, rope, mstate, cache_latent, cache_krope_t, page_table, state_C, state_n,
                                   state_m, state_pool, g_mix, w_in, kv_norm_g, w_ukv, w_pool, pool_scale, b_i, b_f,
                                   mlstm_norm_g, w_proj_attn, w_proj_pool, w_proj_mlstm, w_out, g_mlp, w_up, w_down)
        p_states.append(ps)
        s_states.append(ss)
    y_prompt = _rmsnorm(x, g_final, F32, tm=512, rows=NP).reshape(BATCH, SEQ, D_MODEL).astype(x_prompt.dtype)
    y_sample = _rmsnorm(x[NP:], g_final, F32, tm=NS).reshape(DEC_BATCH, DEC_SEQ, D_MODEL).astype(x_sample.dtype)

    def stack(states, i, like):
        return jnp.stack([s[i] for s in states]).astype(like.dtype)

    (p_c, p_n, p_m), (s_c, s_n, s_m) = mstate
    return (y_prompt, y_sample,
            stack(p_states, 0, cache_latent), stack(p_states, 1, cache_krope),
            p_c.astype(state_C.dtype), p_n.astype(state_n.dtype), p_m[..., 0].astype(state_m.dtype),
            stack(p_states, 2, state_pool),
            stack(s_states, 0, cache_latent), stack(s_states, 1, cache_krope),
            s_c.astype(state_C.dtype), s_n.astype(state_n.dtype), s_m[..., 0].astype(state_m.dtype),
            stack(s_states, 2, state_pool))
```

```python
import functools

import numpy as np
import jax
import jax.numpy as jnp
from jax import lax
from jax.experimental import pallas as pl
from jax.experimental.pallas import tpu as pltpu

D_MODEL = 2048
BATCH = 2
SEQ = 4096
DEPTH = 2
DEC_BATCH = 128
DEC_SEQ = 4
PAST_LEN = 8192
PAGE_SIZE = 128
N_PAGES = PAST_LEN // PAGE_SIZE

MLA_HEADS = 16
QK_NOPE = 128
QK_ROPE = 64
V_HEAD = 128
KV_RANK = 512
ROPE_THETA = 10000.0
ATTN_SCALE = (QK_NOPE + QK_ROPE) ** -0.5
LOG2E = 1.4426950408889634
Q_COLS = MLA_HEADS * (QK_NOPE + QK_ROPE)
KV_COLS = KV_RANK + QK_ROPE

POOL_WINDOWS = (2, 4, 8, 16)
POOL_W = 2048
POOL_GROUP = POOL_W // len(POOL_WINDOWS)
POOL_CTX = max(POOL_WINDOWS) - 1

MLSTM_HEADS = 8
MLSTM_DH = 256
MLSTM_W = MLSTM_HEADS * MLSTM_DH
N_BRANCH = 3
D_FF = 4 * D_MODEL
EPS = 1e-6

NP = BATCH * SEQ
NS = DEC_BATCH * DEC_SEQ
NT = NP + NS

LANE = 128
QH = 2 * LANE
KVP = KV_RANK + LANE
NEG = -1e30
ONES_ROWS = 16
VMEM_LIMIT = 56 * 1024 * 1024

F32 = jnp.float32
BF16 = jnp.bfloat16


def _params(sem, vmem=VMEM_LIMIT):
    return pltpu.CompilerParams(dimension_semantics=sem, vmem_limit_bytes=vmem)


def _nt_dot(a, b):
    return lax.dot_general(a, b, (((1,), (1,)), ((), ())), preferred_element_type=F32)


def _tn_dot(a, b):
    return lax.dot_general(a, b, (((0,), (0,)), ((), ())), preferred_element_type=F32)


def _rmsnorm_kernel(x_ref, g_ref, o_ref):
    x = x_ref[...]
    ms = jnp.mean(x * x, axis=-1, keepdims=True)
    o_ref[...] = (x * lax.rsqrt(ms + EPS) * g_ref[...]).astype(o_ref.dtype)


def _rmsnorm(x, g, out_dtype, tm=544, rows=None):
    m = rows or x.shape[0]
    d = x.shape[1]
    assert m % tm == 0, (m, tm)
    return pl.pallas_call(
        _rmsnorm_kernel,
        grid=(m // tm,),
        in_specs=[pl.BlockSpec((tm, d), lambda i: (i, 0)), pl.BlockSpec((1, d), lambda i: (0, 0))],
        out_specs=pl.BlockSpec((tm, d), lambda i: (i, 0)),
        out_shape=jax.ShapeDtypeStruct((m, d), out_dtype),
        compiler_params=_params(("parallel",)),
        name="rmsnorm",
    )(x, g.reshape(1, d))


def _matmul_kernel(*refs, n_extra, nk, epilogue):
    x_ref, w_ref = refs[0], refs[1]
    extra = refs[2:2 + n_extra]
    o_ref = refs[2 + n_extra]
    part = jnp.dot(x_ref[...].astype(BF16), w_ref[...], preferred_element_type=F32)
    if nk == 1:
        o_ref[...] = epilogue(part, *extra).astype(o_ref.dtype)
        return
    acc_ref = refs[3 + n_extra]
    k = pl.program_id(2)

    @pl.when(k == 0)
    def _():
        acc_ref[...] = part

    @pl.when(k > 0)
    def _():
        acc_ref[...] += part

    @pl.when(k == nk - 1)
    def _():
        o_ref[...] = epilogue(acc_ref[...], *extra).astype(o_ref.dtype)


def _matmul(x, w, *, tm, tn, out_dtype, name, tk=None, rows=None, layer=None, epilogue=None, extras=()):
    m = rows or x.shape[0]
    kdim, n = w.shape[-2:]
    tk = tk or kdim
    nk = kdim // tk
    assert m % tm == 0 and n % tn == 0 and kdim % tk == 0, (m, n, kdim, tm, tn, tk)
    if epilogue is None:
        epilogue = lambda acc: acc
    if layer is None:
        w_spec = pl.BlockSpec((tk, tn), lambda i, j, k: (k, j))
    else:
        w_spec = pl.BlockSpec((None, tk, tn), lambda i, j, k: (layer, k, j))
    kern = functools.partial(_matmul_kernel, n_extra=len(extras), nk=nk, epilogue=epilogue)
    return pl.pallas_call(
        kern,
        grid=(m // tm, n // tn, nk),
        in_specs=[pl.BlockSpec((tm, tk), lambda i, j, k: (i, k)), w_spec] + [s for _, s in extras],
        out_specs=pl.BlockSpec((tm, tn), lambda i, j, k: (i, j)),
        out_shape=jax.ShapeDtypeStruct((m, n), out_dtype),
        scratch_shapes=[pltpu.VMEM((tm, tn), F32)] if nk > 1 else [],
        compiler_params=_params(("parallel", "parallel", "arbitrary")),
        name=name,
    )(x, w, *[a for a, _ in extras])


def _blockdiag_kernel(x_ref, w_ref, o_ref):
    o_ref[...] = jnp.dot(x_ref[...], w_ref[...], preferred_element_type=F32).astype(o_ref.dtype)


def _blockdiag_matmul(x, w, out_dtype, name):
    m = x.shape[0]
    g, kg, ng = w.shape
    return pl.pallas_call(
        _blockdiag_kernel,
        grid=(g,),
        in_specs=[pl.BlockSpec((m, kg), lambda i: (0, i)), pl.BlockSpec((None, kg, ng), lambda i: (i, 0, 0))],
        out_specs=pl.BlockSpec((m, ng), lambda i: (0, i)),
        out_shape=jax.ShapeDtypeStruct((m, g * ng), out_dtype),
        compiler_params=_params(("parallel",)),
        name=name,
    )(x, w)


def _rope_lanes(pe, c_ref, s1_ref, s2_ref):
    return (pe * c_ref[...] + pltpu.roll(pe, LANE - QK_ROPE // 2, 1) * s1_ref[...]
            + pltpu.roll(pe, QK_ROPE // 2, 1) * s2_ref[...])


def _rope_tables():
    pos = jnp.concatenate([jnp.tile(jnp.arange(SEQ, dtype=F32), BATCH),
                           jnp.tile(PAST_LEN + jnp.arange(DEC_SEQ, dtype=F32), DEC_BATCH)])
    inv = ROPE_THETA ** (-jnp.arange(0, QK_ROPE, 2, dtype=F32) / QK_ROPE)
    ang = pos[:, None] * inv[None, :]
    cos, sin = jnp.cos(ang), jnp.sin(ang)
    z = jnp.zeros_like(cos)
    c = jnp.concatenate([cos, cos, z, z], axis=1)
    s1 = jnp.concatenate([-sin, z, z, z], axis=1)
    s2 = jnp.concatenate([z, sin, z, z], axis=1)
    return c, s1, s2


def _q_epilogue(acc, c_ref, s1_ref, s2_ref):
    parts = []
    for h in range(acc.shape[1] // QH):
        parts.append(acc[:, h * QH:h * QH + LANE])
        parts.append(_rope_lanes(acc[:, h * QH + LANE:(h + 1) * QH], c_ref, s1_ref, s2_ref))
    return jnp.concatenate(parts, axis=1) * (ATTN_SCALE * LOG2E)


def _kv_epilogue(acc, g_ref, c_ref, s1_ref, s2_ref):
    lat = acc[:, :KV_RANK]
    ms = jnp.mean(lat * lat, axis=-1, keepdims=True)
    lat = lat * lax.rsqrt(ms + EPS) * g_ref[...]
    return jnp.concatenate([lat, _rope_lanes(acc[:, KV_RANK:], c_ref, s1_ref, s2_ref)], axis=1)


def _vt_kernel(w_ref, x_ref, o_ref):
    o_ref[...] = _nt_dot(w_ref[...], x_ref[...].astype(BF16)).astype(o_ref.dtype)


def _vt_matmul(w_t, x, *, rows, tt, tn=1024):
    n, kdim = w_t.shape
    return pl.pallas_call(
        _vt_kernel,
        grid=(rows // tt, n // tn),
        in_specs=[pl.BlockSpec((tn, kdim), lambda j, i: (i, 0)), pl.BlockSpec((tt, kdim), lambda j, i: (j, 0))],
        out_specs=pl.BlockSpec((None, tn, tt), lambda j, i: (j, i, 0)),
        out_shape=jax.ShapeDtypeStruct((rows // tt, n, tt), BF16),
        compiler_params=_params(("parallel", "parallel")),
        name="v_up_t",
    )(w_t, x)


def _attn_block(iq, q_ref, k_ref, vt_ref, o_ref, s0_ref, s1_ref, m_ref, acc_ref, tq):
    q = q_ref[...]
    m_ref[...] = jnp.full_like(m_ref, NEG)
    acc_ref[...] = jnp.zeros_like(acc_ref)

    def scores(ik, s_ref):
        start = pl.multiple_of(ik * tq, tq)
        s_ref[...] = _nt_dot(k_ref[pl.ds(start, tq), :], q)

    def update(ik, s_ref, diagonal):
        s = s_ref[...]
        if diagonal:
            ki = lax.broadcasted_iota(jnp.int32, s.shape, 0)
            qi = lax.broadcasted_iota(jnp.int32, s.shape, 1)
            s = jnp.where(ki <= qi, s, NEG)
        m_prev = m_ref[...]
        m_new = jnp.maximum(m_prev, jnp.max(s, axis=0, keepdims=True))
        alpha = jnp.exp2(m_prev - m_new)
        p = jnp.exp2(s - m_new).astype(BF16)
        vt1 = jnp.concatenate([vt_ref[ik], jnp.ones((ONES_ROWS, tq), BF16)], axis=0)
        acc_ref[...] = alpha * acc_ref[...] + jnp.dot(vt1, p, preferred_element_type=F32)
        m_ref[...] = m_new

    scores(0, s0_ref)

    def body(j, carry):
        ik = 2 * j
        scores(ik + 1, s1_ref)
        update(ik, s0_ref, False)
        scores(ik + 2, s0_ref)
        update(ik + 1, s1_ref, False)
        return carry

    lax.fori_loop(0, lax.div(iq, 2), body, 0)

    @pl.when(lax.rem(iq, 2) == 0)
    def _():
        update(iq, s0_ref, True)

    @pl.when(lax.rem(iq, 2) == 1)
    def _():
        scores(iq, s1_ref)
        update(iq - 1, s0_ref, False)
        update(iq, s1_ref, True)

    out = acc_ref[:V_HEAD, :] / acc_ref[V_HEAD:V_HEAD + 1, :]
    o_ref[...] = jnp.transpose(out).astype(o_ref.dtype)


N_SLOTS = 3


def _page_copies(pt_ref, lat_hbm, kr_hbm, lat_buf, kr_buf, sem, d, slot, pg, layer):
    cps = []
    for i in range(pg):
        page = pt_ref[d * pg + i]
        cps.append(pltpu.make_async_copy(lat_hbm.at[layer, page], lat_buf.at[slot, i], sem.at[0, slot]))
        cps.append(pltpu.make_async_copy(kr_hbm.at[layer, page], kr_buf.at[slot, i], sem.at[1, slot]))
    return cps


def _decode_group(q_ref, lat_buf, kr_buf, slot, m_ref, l_ref, acc_ref, pg, n_chains):
    q = q_ref[...]
    q_lat, q_pe = q[:, :KV_RANK], q[:, KV_RANK:]
    per = pg // n_chains
    lats, scores = [], []
    for c in range(n_chains):
        pages = range(c * per, (c + 1) * per)
        lat = jnp.concatenate([lat_buf[slot, i].astype(BF16) for i in pages], axis=0)
        krt = jnp.concatenate([kr_buf[slot, i] for i in pages], axis=1)
        krt = jnp.concatenate([krt, jnp.zeros_like(krt)], axis=0).astype(BF16)
        lats.append(lat)
        scores.append(_nt_dot(q_lat, lat) + jnp.dot(q_pe, krt, preferred_element_type=F32))
    for c in range(n_chains):
        s, m_prev = scores[c], m_ref[c]
        m_new = jnp.maximum(m_prev, jnp.max(s, axis=-1, keepdims=True))
        alpha = jnp.exp2(m_prev - m_new)
        p = jnp.exp2(s - m_new)
        l_ref[c] = alpha * l_ref[c] + jnp.sum(p, axis=-1, keepdims=True)
        acc_ref[c] = alpha * acc_ref[c] + jnp.dot(p.astype(BF16), lats[c], preferred_element_type=F32)
        m_ref[c] = m_new


def _decode_finish(q_ref, new_ref, o_ref, m_ref, l_ref, acc_ref, n_chains):
    m_run = m_ref[0]
    for c in range(1, n_chains):
        m_run = jnp.maximum(m_run, m_ref[c])
    l_run = jnp.zeros_like(m_run)
    acc = jnp.zeros(acc_ref.shape[1:], F32)
    for c in range(n_chains):
        w = jnp.exp2(m_ref[c] - m_run)
        l_run = l_run + w * l_ref[c]
        acc = acc + w * acc_ref[c]
    qf = q_ref[...].astype(F32)
    row = lax.broadcasted_iota(jnp.int32, m_run.shape, 0)
    for t in range(DEC_SEQ):
        kn = new_ref[t:t + 1, :]
        st = jnp.sum(qf * kn, axis=-1, keepdims=True)
        st = jnp.where(row >= t * MLA_HEADS, st, NEG)
        m_nxt = jnp.maximum(m_run, st)
        a = jnp.exp2(m_run - m_nxt)
        pt = jnp.exp2(st - m_nxt)
        l_run = a * l_run + pt
        acc = a * acc + pt * kn[:, :KV_RANK]
        m_run = m_nxt
    o_ref[...] = (acc / l_run).astype(o_ref.dtype)


def _attention_kernel(pt_ref, q_ref, k_ref, vt_ref, qd_ref, new_ref, lat_hbm, kr_hbm, o_ref, od_ref,
                      s0_ref, s1_ref, m_ref, acc_ref, lat_buf, kr_buf, sem, md_ref, ld_ref, accd_ref,
                      *, tq, pg, n_chains, layer, n_steps):
    f = (pl.program_id(0) * pl.num_programs(1) + pl.program_id(1)) * pl.num_programs(2) + pl.program_id(2)
    d_a, d_b = 2 * f, 2 * f + 1

    def copies(d):
        return _page_copies(pt_ref, lat_hbm, kr_hbm, lat_buf, kr_buf, sem, d, lax.rem(d, N_SLOTS), pg, layer)

    @pl.when(f == 0)
    def _():
        for cp in copies(d_a):
            cp.start()

    for cp in copies(d_b):
        cp.start()

    _attn_block(pl.program_id(2), q_ref, k_ref, vt_ref, o_ref, s0_ref, s1_ref, m_ref, acc_ref, tq)

    @pl.when(f + 1 < n_steps)
    def _():
        for cp in copies(d_a + 2):
            cp.start()

    @pl.when(lax.rem(f, 2) == 0)
    def _():
        md_ref[...] = jnp.full_like(md_ref, NEG)
        ld_ref[...] = jnp.zeros_like(ld_ref)
        accd_ref[...] = jnp.zeros_like(accd_ref)

    for d in (d_a, d_b):
        for cp in copies(d):
            cp.wait()
        _decode_group(qd_ref, lat_buf, kr_buf, lax.rem(d, N_SLOTS), md_ref, ld_ref, accd_ref, pg, n_chains)

    @pl.when(lax.rem(f, 2) == 1)
    def _():
        _decode_finish(qd_ref, new_ref, od_ref, md_ref, ld_ref, accd_ref, n_chains)


def _attention(q, k, vt, q_abs, ckv_new, cache_latent, cache_krope_t, page_table, layer, *,
               batch, seq, heads, tq, out_rows, pg=16, n_chains=4):
    nq = seq // tq
    n_steps = batch * heads * nq
    rows = DEC_SEQ * MLA_HEADS
    assert N_PAGES == 4 * pg and 2 * n_steps == DEC_BATCH * (N_PAGES // pg), "two page groups per prompt grid step"

    def seq_of(b, h, i):
        return ((b * heads + h) * nq + i) // 2

    grid_spec = pltpu.PrefetchScalarGridSpec(
        num_scalar_prefetch=1,
        grid=(batch, heads, nq),
        in_specs=[pl.BlockSpec((tq, QH), lambda b, h, i, pt: (b * nq + i, h)),
                  pl.BlockSpec((seq, QH), lambda b, h, i, pt: (b, h)),
                  pl.BlockSpec((nq, V_HEAD, tq), lambda b, h, i, pt: (b, h, 0)),
                  pl.BlockSpec((None, rows, KVP), lambda b, h, i, pt: (seq_of(b, h, i), 0, 0)),
                  pl.BlockSpec((None, DEC_SEQ, KVP), lambda b, h, i, pt: (seq_of(b, h, i), 0, 0)),
                  pl.BlockSpec(memory_space=pl.ANY),
                  pl.BlockSpec(memory_space=pl.ANY)],
        out_specs=[pl.BlockSpec((tq, V_HEAD), lambda b, h, i, pt: (b * nq + i, h)),
                   pl.BlockSpec((None, rows, KV_RANK), lambda b, h, i, pt: (seq_of(b, h, i), 0, 0))],
        scratch_shapes=[pltpu.VMEM((tq, tq), F32), pltpu.VMEM((tq, tq), F32),
                        pltpu.VMEM((1, tq), F32), pltpu.VMEM((V_HEAD + ONES_ROWS, tq), F32),
                        pltpu.VMEM((N_SLOTS, pg, PAGE_SIZE, KV_RANK), F32),
                        pltpu.VMEM((N_SLOTS, pg, QK_ROPE, PAGE_SIZE), F32),
                        pltpu.SemaphoreType.DMA((2, N_SLOTS)),
                        pltpu.VMEM((n_chains, rows, 1), F32), pltpu.VMEM((n_chains, rows, 1), F32),
                        pltpu.VMEM((n_chains, rows, KV_RANK), F32)],
    )
    return pl.pallas_call(
        functools.partial(_attention_kernel, tq=tq, pg=pg, n_chains=n_chains, layer=layer, n_steps=n_steps),
        grid_spec=grid_spec,
        out_shape=[jax.ShapeDtypeStruct((out_rows, heads * V_HEAD), BF16),
                   jax.ShapeDtypeStruct((DEC_BATCH, rows, KV_RANK), BF16)],
        compiler_params=_params(("arbitrary", "arbitrary", "arbitrary")),
        name="attention",
    )(page_table.reshape(-1), q, k, vt, q_abs, ckv_new, cache_latent, cache_krope_t)


def _pool_prompt_kernel(u_ref, halo_ref, maps_ref, scale_ref, o_ref, *, tm):
    row0 = lax.rem(pl.program_id(0) * tm, SEQ)
    pos1 = row0 + 1 + lax.broadcasted_iota(jnp.int32, (tm, 1), 0)
    for g, w in enumerate(POOL_WINDOWS):
        sl = slice(g * POOL_GROUP, (g + 1) * POOL_GROUP)
        u = u_ref[:, sl]
        halo = jnp.where(row0 != 0, halo_ref[:, sl], 0.0)
        acc = jnp.concatenate([halo, u], axis=0)
        shift = 1
        while shift < w:
            acc = acc + pltpu.roll(acc, shift, 0)
            shift *= 2
        cnt = jnp.minimum(w, pos1).astype(F32)
        d = acc[POOL_CTX + 1:, :] / cnt - u
        y = jnp.dot(d.astype(BF16), maps_ref[g], preferred_element_type=F32) * scale_ref[:, sl]
        o_ref[:, sl] = y.astype(o_ref.dtype)


def _pool_prompt(u, maps, scale, tm=512):
    hb = POOL_CTX + 1
    return pl.pallas_call(
        functools.partial(_pool_prompt_kernel, tm=tm),
        grid=(NP // tm,),
        in_specs=[pl.BlockSpec((tm, POOL_W), lambda i: (i, 0)),
                  pl.BlockSpec((hb, POOL_W), lambda i: (jnp.maximum(i * (tm // hb) - 1, 0), 0)),
                  pl.BlockSpec(maps.shape, lambda i: (0, 0, 0)),
                  pl.BlockSpec((1, POOL_W), lambda i: (0, 0))],
        out_specs=pl.BlockSpec((tm, POOL_W), lambda i: (i, 0)),
        out_shape=jax.ShapeDtypeStruct((NT, POOL_W), BF16),
        compiler_params=_params(("parallel",)),
        name="pool_prompt",
    )(u, u, maps, scale)


def _pool_sample_kernel(ctx_ref, maps_ref, scale_ref, o_ref):
    for g, w in enumerate(POOL_WINDOWS):
        sl = slice(g * POOL_GROUP, (g + 1) * POOL_GROUP)
        rows = []
        for t in range(DEC_SEQ):
            r = POOL_CTX + t
            acc = ctx_ref[r, :, sl]
            for j in range(1, w):
                acc = acc + ctx_ref[r - j, :, sl]
            cnt = float(min(w, PAST_LEN - POOL_CTX + r + 1))
            rows.append(acc / cnt - ctx_ref[r, :, sl])
        d = jnp.concatenate(rows, axis=0).astype(BF16)
        y = jnp.dot(d, maps_ref[g], preferred_element_type=F32) * scale_ref[:, sl]
        bb = y.shape[0] // DEC_SEQ
        for t in range(DEC_SEQ):
            o_ref[t, :, sl] = y[t * bb:(t + 1) * bb].astype(o_ref.dtype)


def _pool_sample(ctx_t, maps, scale, bb=32):
    tc = POOL_CTX + DEC_SEQ
    return pl.pallas_call(
        _pool_sample_kernel,
        grid=(DEC_BATCH // bb,),
        in_specs=[pl.BlockSpec((tc, bb, POOL_W), lambda i: (0, i, 0)),
                  pl.BlockSpec(maps.shape, lambda i: (0, 0, 0)),
                  pl.BlockSpec((1, POOL_W), lambda i: (0, 0))],
        out_specs=pl.BlockSpec((DEC_SEQ, bb, POOL_W), lambda i: (0, i, 0)),
        out_shape=jax.ShapeDtypeStruct((DEC_SEQ, DEC_BATCH, POOL_W), BF16),
        compiler_params=_params(("parallel",)),
        name="pool_sample",
    )(ctx_t, maps, scale)


def _pad_rows(x, n):
    if x.shape[0] == n:
        return x
    return jnp.concatenate([x, jnp.zeros((n - x.shape[0], x.shape[1]), x.dtype)], axis=0)


def _mlstm_kernel(q_ref, k_ref, v_ref, og_ref, zt_ref, bias_ref, g_ref, c0_ref, n0_ref, m0_ref, *refs, L, rows):
    h_ref, c_ref, n_ref, m_ref = refs[-4:]

    @pl.when(pl.program_id(1) == 0)
    def _():
        c_ref[...] = c0_ref[...]
        n_ref[...] = n0_ref[...]
        m_ref[...] = m0_ref[...]

    zb = zt_ref[...] + bias_ref[...]
    li_all = zb[:MLSTM_HEADS]
    fx = zb[MLSTM_HEADS:]
    lf_all = jnp.minimum(fx, 0.0) - jnp.log1p(jnp.exp(-jnp.abs(fx)))
    ri = lax.broadcasted_iota(jnp.int32, (L, L), 0)
    ci = lax.broadcasted_iota(jnp.int32, (L, L), 1)
    eye = ri == ci
    tril = ci <= ri
    triu = ri <= ci

    for h in range(MLSTM_HEADS):
        hs = slice(h * MLSTM_DH, (h + 1) * MLSTM_DH)
        qc = _pad_rows(q_ref[:, hs], L)
        kc = _pad_rows(k_ref[:, hs], L) * jnp.asarray(MLSTM_DH ** -0.5, BF16)
        vc = _pad_rows(v_ref[:, hs], L)
        li_r = li_all[h:h + 1, :]
        lf_r = lf_all[h:h + 1, :]
        li_c = jnp.sum(jnp.where(eye, li_r, 0.0), axis=1, keepdims=True)
        lf_c = jnp.sum(jnp.where(eye, lf_r, 0.0), axis=1, keepdims=True)
        bcum_c = jnp.sum(jnp.where(tril, lf_r, 0.0), axis=1, keepdims=True)
        bcum_r = jnp.sum(jnp.where(triu, lf_c, 0.0), axis=0, keepdims=True)
        m_prev = m_ref[h:h + 1, :]
        c_prev = c_ref[h]
        n_prev = n_ref[h:h + 1, :]

        dmat = jnp.where(tril, bcum_c - bcum_r + li_r, NEG)
        m_t = jnp.maximum(bcum_c + m_prev, jnp.max(dmat, axis=1, keepdims=True))
        inter = jnp.exp(bcum_c + m_prev - m_t)
        wts = jnp.exp(dmat - m_t)
        s = _nt_dot(qc, kc) * wts
        num = inter * jnp.dot(qc, c_prev.astype(BF16), preferred_element_type=F32) \
            + jnp.dot(s.astype(BF16), vc, preferred_element_type=F32)
        qn = jnp.sum(qc.astype(F32) * n_prev, axis=1, keepdims=True)
        den = inter * qn + jnp.sum(s, axis=1, keepdims=True)
        hout = num / jnp.maximum(jnp.abs(den), jnp.exp(-m_t))
        hout = hout[:rows]
        ms = jnp.mean(hout * hout, axis=-1, keepdims=True)
        hn = hout * lax.rsqrt(ms + EPS) * g_ref[:, hs]
        gate = jax.nn.sigmoid(og_ref[:, hs].astype(F32))
        h_ref[:, hs] = (hn * gate).astype(h_ref.dtype)

        m_last = m_t[L - 1:L, :]
        b_last = bcum_c[L - 1:L, :]
        w_last = jnp.exp(b_last - bcum_c + li_c - m_last)
        decay = jnp.exp(b_last + m_prev - m_last)
        kw = kc.astype(F32) * w_last
        c_ref[h] = decay * c_prev + _tn_dot(kw.astype(BF16), vc)
        n_ref[h:h + 1, :] = decay * n_prev + jnp.sum(kw, axis=0, keepdims=True)
        m_ref[h:h + 1, :] = m_last


def _mlstm(mqkvo, zt, bias, gain, init, init_layer, prev, layer, *, nb, n_chunks, rows, L, out_rows):
    def tok_spec(col):
        return pl.BlockSpec((rows, MLSTM_W), lambda b, c: (b * n_chunks + c, col))

    def state_specs(li):
        return [pl.BlockSpec((None, None, MLSTM_HEADS, MLSTM_DH, MLSTM_DH), lambda b, c: (li, b, 0, 0, 0)),
                pl.BlockSpec((None, None, MLSTM_HEADS, MLSTM_DH), lambda b, c: (li, b, 0, 0)),
                pl.BlockSpec((None, None, MLSTM_HEADS, 1), lambda b, c: (li, b, 0, 0))]

    prev = tuple(prev) if prev is not None else ()
    n_in = 10
    return pl.pallas_call(
        functools.partial(_mlstm_kernel, L=L, rows=rows),
        grid=(nb, n_chunks),
        in_specs=[tok_spec(0), tok_spec(1), tok_spec(2), tok_spec(3),
                  pl.BlockSpec((None, 2 * MLSTM_HEADS, L), lambda b, c: (b, 0, c)),
                  pl.BlockSpec((2 * MLSTM_HEADS, 1), lambda b, c: (0, 0)),
                  pl.BlockSpec((1, MLSTM_W), lambda b, c: (0, 0))] + state_specs(init_layer)
        + [pl.BlockSpec(memory_space=pl.ANY)] * len(prev),
        out_specs=[pl.BlockSpec((rows, MLSTM_W), lambda b, c: (b * n_chunks + c, 0))] + state_specs(layer),
        out_shape=[jax.ShapeDtypeStruct((out_rows, MLSTM_W), BF16),
                   jax.ShapeDtypeStruct((DEPTH, nb, MLSTM_HEADS, MLSTM_DH, MLSTM_DH), F32),
                   jax.ShapeDtypeStruct((DEPTH, nb, MLSTM_HEADS, MLSTM_DH), F32),
                   jax.ShapeDtypeStruct((DEPTH, nb, MLSTM_HEADS, 1), F32)],
        input_output_aliases={n_in + i: 1 + i for i in range(len(prev))},
        compiler_params=_params(("parallel", "arbitrary")),
        name="mlstm",
    )(mqkvo, mqkvo, mqkvo, mqkvo, zt, bias, gain, *init, *prev)


def _merge_kernel(xa_ref, xb_ref, xc_ref, wa_ref, wb_ref, wc_ref, ga_ref, gb_ref, gc_ref, o_ref):
    out = None
    for x_ref, w_ref, g_ref in ((xa_ref, wa_ref, ga_ref), (xb_ref, wb_ref, gb_ref), (xc_ref, wc_ref, gc_ref)):
        br = jnp.dot(x_ref[...], w_ref[...], preferred_element_type=F32)
        term = jax.nn.sigmoid(g_ref[...].astype(F32)) * br
        out = term if out is None else out + term
    o_ref[...] = out.astype(o_ref.dtype)


def _merge(xs, ws, zg, layer, tm=1088, tn=512):
    nj = D_MODEL // tn
    assert NT % tm == 0 and D_MODEL % tn == 0
    x_spec = pl.BlockSpec((tm, D_MODEL), lambda i, j: (i, 0))
    w_spec = pl.BlockSpec((None, D_MODEL, tn), lambda i, j: (layer, 0, j))
    return pl.pallas_call(
        _merge_kernel,
        grid=(NT // tm, nj),
        in_specs=[x_spec] * N_BRANCH + [w_spec] * N_BRANCH
        + [pl.BlockSpec((tm, tn), functools.partial(lambda i, j, b: (i, b * nj + j), b=b)) for b in range(N_BRANCH)],
        out_specs=pl.BlockSpec((tm, tn), lambda i, j: (i, j)),
        out_shape=jax.ShapeDtypeStruct((NT, D_MODEL), BF16),
        compiler_params=_params(("parallel", "parallel")),
        name="branch_merge",
    )(*xs, *ws, zg, zg, zg)


def _add_residual(acc, r_ref):
    return acc + r_ref[...]


def _relu2(acc):
    return jnp.square(jnp.maximum(acc, 0.0))


def _layer(x, l, rope, mstate, cache_latent, cache_krope_t, page_table, state_C, state_n, state_m, state_pool,
           g_mix, w_in, kv_norm_g, w_ukv, w_pool, pool_scale, b_i, b_f, mlstm_norm_g,
           w_proj_attn, w_proj_pool, w_proj_mlstm, w_out, g_mlp, w_up, w_down):
    c_tab, s1_tab, s2_tab = rope
    tm = 1088
    rope_specs = [(t, pl.BlockSpec((tm, LANE), lambda i, j, k: (i, 0))) for t in (c_tab, s1_tab, s2_tab)]

    wl = w_in[l]
    o = np.cumsum((0, Q_COLS, KV_COLS, POOL_W, 4 * MLSTM_W, 2 * MLSTM_HEADS, N_BRANCH * D_MODEL)).tolist()
    wq = wl[:, o[0]:o[1]].reshape(D_MODEL, MLA_HEADS, QK_NOPE + QK_ROPE)
    wq = jnp.pad(wq, ((0, 0), (0, 0), (0, QH - QK_NOPE - QK_ROPE))).reshape(D_MODEL, MLA_HEADS * QH).astype(BF16)
    wkv = jnp.pad(wl[:, o[1]:o[2]], ((0, 0), (0, KVP - KV_COLS))).astype(BF16)
    wu = wl[:, o[2]:o[3]].astype(BF16)
    wm = wl[:, o[3]:o[4]].astype(BF16)
    wif = jnp.pad(wl[:, o[4]:o[5]], ((0, 0), (0, LANE - 2 * MLSTM_HEADS))).astype(BF16)
    wg = wl[:, o[5]:o[6]].astype(BF16)

    w_uk = w_ukv[l][..., :QK_NOPE]
    w_uv = w_ukv[l][..., QK_NOPE:]
    eye_r = jnp.eye(QK_ROPE, dtype=F32)
    k_top = jnp.pad(w_uk, ((0, 0), (0, 0), (0, QH - QK_NOPE)))
    k_bot = jnp.broadcast_to(jnp.pad(eye_r, ((0, LANE - QK_ROPE), (QK_NOPE, QH - QK_NOPE - QK_ROPE)))[:, None, :],
                             (LANE, MLA_HEADS, QH))
    w_kb = jnp.concatenate([k_top, k_bot], axis=0).reshape(KVP, MLA_HEADS * QH).astype(BF16)
    w_vt = jnp.pad(jnp.transpose(w_uv, (1, 2, 0)).reshape(MLA_HEADS * V_HEAD, KV_RANK),
                   ((0, 0), (0, KVP - KV_RANK))).astype(BF16)
    a_top = jnp.pad(jnp.transpose(w_uk, (1, 2, 0)), ((0, 0), (0, 0), (0, KVP - KV_RANK)))
    a_bot = jnp.broadcast_to(jnp.pad(eye_r, ((0, QH - QK_NOPE - QK_ROPE), (KV_RANK, KVP - KV_RANK - QK_ROPE)))[None],
                             (MLA_HEADS, LANE, KVP))
    w_abs = jnp.concatenate([a_top, a_bot], axis=1).astype(BF16)
    w_uvh = jnp.transpose(w_uv, (1, 0, 2)).astype(BF16)

    h = _rmsnorm(x, g_mix[l], BF16)

    q = _matmul(h, wq, tm=tm, tn=4 * QH, out_dtype=BF16, name="q_proj", epilogue=_q_epilogue, extras=rope_specs)
    ckv = _matmul(h, wkv, tm=tm, tn=KVP, out_dtype=F32, name="kv_proj", epilogue=_kv_epilogue,
                  extras=[(kv_norm_g[l].reshape(1, KV_RANK), pl.BlockSpec((1, KV_RANK), lambda i, j, k: (0, 0)))]
                  + rope_specs)
    tq = 512
    k_p = _matmul(ckv, w_kb, rows=NP, tm=1024, tn=1024, out_dtype=BF16, name="k_up")
    vt_p = _vt_matmul(w_vt, ckv, rows=NP, tt=tq)
    q_abs = _blockdiag_matmul(q[NP:], w_abs, BF16, "q_absorb")
    o_p, o_lat = _attention(q, k_p, vt_p, q_abs.reshape(DEC_BATCH, DEC_SEQ * MLA_HEADS, KVP),
                            ckv[NP:].reshape(DEC_BATCH, DEC_SEQ, KVP), cache_latent, cache_krope_t, page_table, l,
                            batch=BATCH, seq=SEQ, heads=MLA_HEADS, tq=tq, out_rows=NT)
    o_s = _blockdiag_matmul(o_lat.reshape(NS, MLA_HEADS * KV_RANK), w_uvh, BF16, "v_up_sample")
    o_attn = o_p.at[NP:].set(o_s)

    u = _matmul(h, wu, tm=2 * tm, tn=512, out_dtype=F32, name="pool_in")
    maps = w_pool[l].astype(BF16)
    scale = pool_scale[l].reshape(1, POOL_W)
    u_s = u[NP:].reshape(DEC_BATCH, DEC_SEQ, POOL_W)
    ctx_s = jnp.concatenate([state_pool[l], u_s], axis=1)
    pm_s = _pool_sample(jnp.transpose(ctx_s, (1, 0, 2)), maps, scale)
    pm = _pool_prompt(u, maps, scale).at[NP:].set(jnp.transpose(pm_s, (1, 0, 2)).reshape(NS, POOL_W))

    mz = _matmul(h, wm, tm=2 * tm, tn=1024, out_dtype=BF16, name="mlstm_in")
    zif = _matmul(h, wif, tm=tm, tn=LANE, out_dtype=F32, name="gate_in")[:, :2 * MLSTM_HEADS]
    bias = jnp.concatenate([b_i[l], b_f[l]]).reshape(2 * MLSTM_HEADS, 1)
    gain = mlstm_norm_g[l].reshape(1, MLSTM_W)
    lp, ls, rs = 256, LANE, 16
    zt_p = jnp.transpose(zif[:NP].reshape(BATCH, SEQ, 2 * MLSTM_HEADS), (0, 2, 1))
    zero_c = jnp.zeros((1, BATCH, MLSTM_HEADS, MLSTM_DH, MLSTM_DH), F32)
    p_prev, s_prev = mstate if mstate is not None else (None, None)
    hm_p, *p_new = _mlstm(mz, zt_p, bias, gain, (zero_c, zero_c[..., 0], zero_c[..., :1, 0]), 0, p_prev, l,
                          nb=BATCH, n_chunks=SEQ // lp, rows=lp, L=lp, out_rows=NT)
    mz_s = jnp.pad(mz[NP:].reshape(DEC_BATCH, DEC_SEQ, 4 * MLSTM_W), ((0, 0), (0, rs - DEC_SEQ), (0, 0)))
    zt_s = jnp.transpose(zif[NP:].reshape(DEC_BATCH, DEC_SEQ, 2 * MLSTM_HEADS), (0, 2, 1))
    pad_i = jnp.full((DEC_BATCH, MLSTM_HEADS, ls - DEC_SEQ), NEG, F32)
    zt_s = jnp.concatenate([zt_s, jnp.concatenate([pad_i, -pad_i], axis=1)], axis=2)
    hm_s, *s_new = _mlstm(mz_s.reshape(DEC_BATCH * rs, 4 * MLSTM_W), zt_s, bias, gain,
                          (state_C, state_n, state_m[..., None]), l, s_prev, l,
                          nb=DEC_BATCH, n_chunks=1, rows=rs, L=ls, out_rows=DEC_BATCH * rs)
    hm_s = hm_s.reshape(DEC_BATCH, rs, MLSTM_W)[:, :DEC_SEQ].reshape(NS, MLSTM_W)
    hm = hm_p.at[NP:].set(hm_s)

    zg = _matmul(h, wg, tm=2 * tm, tn=1024, out_dtype=BF16, name="branch_gates")
    merged = _merge((o_attn, pm, hm), [w.astype(BF16) for w in (w_proj_attn, w_proj_pool, w_proj_mlstm)], zg, l)
    x = _matmul(merged, w_out.astype(BF16), layer=l, tm=tm, tn=1024, out_dtype=F32, name="out_proj",
                epilogue=_add_residual, extras=[(x, pl.BlockSpec((tm, 1024), lambda i, j, k: (i, j)))])

    hf = _rmsnorm(x, g_mlp[l], BF16)
    a = _matmul(hf, w_up.astype(BF16), layer=l, tm=2 * tm, tn=1024, out_dtype=BF16, name="mlp_up", epilogue=_relu2)
    x = _matmul(a, w_down.astype(BF16), layer=l, tm=tm // 2, tn=512, out_dtype=F32, name="mlp_down",
                epilogue=_add_residual, extras=[(x, pl.BlockSpec((tm // 2, 512), lambda i, j, k: (i, j)))])

    p_state = (ckv[:NP, :KV_RANK].reshape(BATCH, SEQ, KV_RANK),
               ckv[:NP, KV_RANK:KV_COLS].reshape(BATCH, SEQ, QK_ROPE),
               jnp.stack([u[(b + 1) * SEQ - POOL_CTX:(b + 1) * SEQ] for b in range(BATCH)]))
    s_state = (ckv[NP:, :KV_RANK].reshape(DEC_BATCH, DEC_SEQ, KV_RANK),
               ckv[NP:, KV_RANK:KV_COLS].reshape(DEC_BATCH, DEC_SEQ, QK_ROPE),
               ctx_s[:, -POOL_CTX:])
    return x, p_state, s_state, (p_new, s_new)


def kernel(x_prompt, x_sample, cache_latent, cache_krope, page_table, state_C, state_n, state_m, state_pool,
           g_mix, w_in, kv_norm_g, w_ukv, w_pool, pool_scale, b_i, b_f, mlstm_norm_g,
           w_proj_attn, w_proj_pool, w_proj_mlstm, w_out, g_mlp, w_up, w_down, g_final):
    x = jnp.concatenate([x_prompt.reshape(NP, D_MODEL), x_sample.reshape(NS, D_MODEL)], axis=0).astype(F32)
    rope = _rope_tables()
    cache_krope_t = jnp.swapaxes(cache_krope, 2, 3)
    p_states, s_states, mstate = [], [], None
    for l in range(DEPTH):
        x, ps, ss, mstate = _layer(x, l, rope, mstate, cache_latent, cache_krope_t, page_table, state_C, state_n,
                                   state_m, state_pool, g_mix, w_in, kv_norm_g, w_ukv, w_pool, pool_scale, b_i, b_f,
                                   mlstm_norm_g, w_proj_attn, w_proj_pool, w_proj_mlstm, w_out, g_mlp, w_up, w_down)
        p_states.append(ps)
        s_states.append(ss)
    y_prompt = _rmsnorm(x, g_final, F32, tm=512, rows=NP).reshape(BATCH, SEQ, D_MODEL).astype(x_prompt.dtype)
    y_sample = _rmsnorm(x[NP:], g_final, F32, tm=NS).reshape(DEC_BATCH, DEC_SEQ, D_MODEL).astype(x_sample.dtype)

    def stack(states, i, like):
        return jnp.stack([s[i] for s in states]).astype(like.dtype)

    (p_c, p_n, p_m), (s_c, s_n, s_m) = mstate
    return (y_prompt, y_sample,
            stack(p_states, 0, cache_latent), stack(p_states, 1, cache_krope),
            p_c.astype(state_C.dtype), p_n.astype(state_n.dtype), p_m[..., 0].astype(state_m.dtype),
            stack(p_states, 2, state_pool),
            stack(s_states, 0, cache_latent), stack(s_states, 1, cache_krope),
            s_c.astype(state_C.dtype), s_n.astype(state_n.dtype), s_m[..., 0].astype(state_m.dtype),
            stack(s_states, 2, state_pool))
```

```python
import functools

import numpy as np
import jax
import jax.numpy as jnp
from jax import lax
from jax.experimental import pallas as pl
from jax.experimental.pallas import tpu as pltpu

D_MODEL = 2048
BATCH = 2
SEQ = 4096
DEPTH = 2
DEC_BATCH = 128
DEC_SEQ = 4
PAST_LEN = 8192
PAGE_SIZE = 128
N_PAGES = PAST_LEN // PAGE_SIZE

MLA_HEADS = 16
QK_NOPE = 128
QK_ROPE = 64
V_HEAD = 128
KV_RANK = 512
ROPE_THETA = 10000.0
ATTN_SCALE = (QK_NOPE + QK_ROPE) ** -0.5
LOG2E = 1.4426950408889634
Q_COLS = MLA_HEADS * (QK_NOPE + QK_ROPE)
KV_COLS = KV_RANK + QK_ROPE

POOL_WINDOWS = (2, 4, 8, 16)
POOL_W = 2048
POOL_GROUP = POOL_W // len(POOL_WINDOWS)
POOL_CTX = max(POOL_WINDOWS) - 1

MLSTM_HEADS = 8
MLSTM_DH = 256
MLSTM_W = MLSTM_HEADS * MLSTM_DH
N_BRANCH = 3
D_FF = 4 * D_MODEL
EPS = 1e-6

NP = BATCH * SEQ
NS = DEC_BATCH * DEC_SEQ
NT = NP + NS

LANE = 128
QH = 2 * LANE
KVP = KV_RANK + LANE
NEG = -1e30
ONES_ROWS = 16
VMEM_LIMIT = 56 * 1024 * 1024

F32 = jnp.float32
BF16 = jnp.bfloat16


def _params(sem, vmem=VMEM_LIMIT):
    return pltpu.CompilerParams(dimension_semantics=sem, vmem_limit_bytes=vmem)


def _nt_dot(a, b):
    return lax.dot_general(a, b, (((1,), (1,)), ((), ())), preferred_element_type=F32)


def _tn_dot(a, b):
    return lax.dot_general(a, b, (((0,), (0,)), ((), ())), preferred_element_type=F32)


def _rmsnorm_kernel(x_ref, g_ref, o_ref):
    x = x_ref[...]
    ms = jnp.mean(x * x, axis=-1, keepdims=True)
    o_ref[...] = (x * lax.rsqrt(ms + EPS) * g_ref[...]).astype(o_ref.dtype)


def _rmsnorm(x, g, out_dtype, tm=544, rows=None):
    m = rows or x.shape[0]
    d = x.shape[1]
    assert m % tm == 0, (m, tm)
    return pl.pallas_call(
        _rmsnorm_kernel,
        grid=(m // tm,),
        in_specs=[pl.BlockSpec((tm, d), lambda i: (i, 0)), pl.BlockSpec((1, d), lambda i: (0, 0))],
        out_specs=pl.BlockSpec((tm, d), lambda i: (i, 0)),
        out_shape=jax.ShapeDtypeStruct((m, d), out_dtype),
        compiler_params=_params(("parallel",)),
        name="rmsnorm",
    )(x, g.reshape(1, d))


def _matmul_kernel(*refs, n_extra, nk, epilogue):
    x_ref, w_ref = refs[0], refs[1]
    extra = refs[2:2 + n_extra]
    o_ref = refs[2 + n_extra]
    part = jnp.dot(x_ref[...].astype(BF16), w_ref[...], preferred_element_type=F32)
    if nk == 1:
        o_ref[...] = epilogue(part, *extra).astype(o_ref.dtype)
        return
    acc_ref = refs[3 + n_extra]
    k = pl.program_id(2)

    @pl.when(k == 0)
    def _():
        acc_ref[...] = part

    @pl.when(k > 0)
    def _():
        acc_ref[...] += part

    @pl.when(k == nk - 1)
    def _():
        o_ref[...] = epilogue(acc_ref[...], *extra).astype(o_ref.dtype)


def _matmul(x, w, *, tm, tn, out_dtype, name, tk=None, rows=None, layer=None, epilogue=None, extras=()):
    m = rows or x.shape[0]
    kdim, n = w.shape[-2:]
    tk = tk or kdim
    nk = kdim // tk
    assert m % tm == 0 and n % tn == 0 and kdim % tk == 0, (m, n, kdim, tm, tn, tk)
    if epilogue is None:
        epilogue = lambda acc: acc
    if layer is None:
        w_spec = pl.BlockSpec((tk, tn), lambda i, j, k: (k, j))
    else:
        w_spec = pl.BlockSpec((None, tk, tn), lambda i, j, k: (layer, k, j))
    kern = functools.partial(_matmul_kernel, n_extra=len(extras), nk=nk, epilogue=epilogue)
    return pl.pallas_call(
        kern,
        grid=(m // tm, n // tn, nk),
        in_specs=[pl.BlockSpec((tm, tk), lambda i, j, k: (i, k)), w_spec] + [s for _, s in extras],
        out_specs=pl.BlockSpec((tm, tn), lambda i, j, k: (i, j)),
        out_shape=jax.ShapeDtypeStruct((m, n), out_dtype),
        scratch_shapes=[pltpu.VMEM((tm, tn), F32)] if nk > 1 else [],
        compiler_params=_params(("parallel", "parallel", "arbitrary")),
        name=name,
    )(x, w, *[a for a, _ in extras])


def _blockdiag_kernel(x_ref, w_ref, o_ref):
    o_ref[...] = jnp.dot(x_ref[...], w_ref[...], preferred_element_type=F32).astype(o_ref.dtype)


def _blockdiag_matmul(x, w, out_dtype, name):
    m = x.shape[0]
    g, kg, ng = w.shape
    return pl.pallas_call(
        _blockdiag_kernel,
        grid=(g,),
        in_specs=[pl.BlockSpec((m, kg), lambda i: (0, i)), pl.BlockSpec((None, kg, ng), lambda i: (i, 0, 0))],
        out_specs=pl.BlockSpec((m, ng), lambda i: (0, i)),
        out_shape=jax.ShapeDtypeStruct((m, g * ng), out_dtype),
        compiler_params=_params(("parallel",)),
        name=name,
    )(x, w)


def _rope_lanes(pe, c_ref, s1_ref, s2_ref):
    return (pe * c_ref[...] + pltpu.roll(pe, LANE - QK_ROPE // 2, 1) * s1_ref[...]
            + pltpu.roll(pe, QK_ROPE // 2, 1) * s2_ref[...])


def _rope_tables():
    pos = jnp.concatenate([jnp.tile(jnp.arange(SEQ, dtype=F32), BATCH),
                           jnp.tile(PAST_LEN + jnp.arange(DEC_SEQ, dtype=F32), DEC_BATCH)])
    inv = ROPE_THETA ** (-jnp.arange(0, QK_ROPE, 2, dtype=F32) / QK_ROPE)
    ang = pos[:, None] * inv[None, :]
    cos, sin = jnp.cos(ang), jnp.sin(ang)
    z = jnp.zeros_like(cos)
    c = jnp.concatenate([cos, cos, z, z], axis=1)
    s1 = jnp.concatenate([-sin, z, z, z], axis=1)
    s2 = jnp.concatenate([z, sin, z, z], axis=1)
    return c, s1, s2


def _q_epilogue(acc, c_ref, s1_ref, s2_ref):
    parts = []
    for h in range(acc.shape[1] // QH):
        parts.append(acc[:, h * QH:h * QH + LANE])
        parts.append(_rope_lanes(acc[:, h * QH + LANE:(h + 1) * QH], c_ref, s1_ref, s2_ref))
    return jnp.concatenate(parts, axis=1) * (ATTN_SCALE * LOG2E)


def _kv_epilogue(acc, g_ref, c_ref, s1_ref, s2_ref):
    lat = acc[:, :KV_RANK]
    ms = jnp.mean(lat * lat, axis=-1, keepdims=True)
    lat = lat * lax.rsqrt(ms + EPS) * g_ref[...]
    return jnp.concatenate([lat, _rope_lanes(acc[:, KV_RANK:], c_ref, s1_ref, s2_ref)], axis=1)


def _vt_kernel(w_ref, x_ref, o_ref):
    o_ref[...] = _nt_dot(w_ref[...], x_ref[...].astype(BF16)).astype(o_ref.dtype)


def _vt_matmul(w_t, x, *, rows, tt, tn=2048):
    n, kdim = w_t.shape
    return pl.pallas_call(
        _vt_kernel,
        grid=(rows // tt, n // tn),
        in_specs=[pl.BlockSpec((tn, kdim), lambda j, i: (i, 0)), pl.BlockSpec((tt, kdim), lambda j, i: (j, 0))],
        out_specs=pl.BlockSpec((None, tn, tt), lambda j, i: (j, i, 0)),
        out_shape=jax.ShapeDtypeStruct((rows // tt, n, tt), BF16),
        compiler_params=_params(("parallel", "parallel")),
        name="v_up_t",
    )(w_t, x)


def _attn_block(iq, q_ref, k_ref, vt_ref, o_ref, s0_ref, s1_ref, m_ref, acc_ref, tq):
    q = q_ref[...]
    m_ref[...] = jnp.full_like(m_ref, NEG)
    acc_ref[...] = jnp.zeros_like(acc_ref)

    def scores(ik, s_ref):
        start = pl.multiple_of(ik * tq, tq)
        s_ref[...] = _nt_dot(k_ref[pl.ds(start, tq), :], q)

    def update(ik, s_ref, diagonal):
        s = s_ref[...]
        if diagonal:
            ki = lax.broadcasted_iota(jnp.int32, s.shape, 0)
            qi = lax.broadcasted_iota(jnp.int32, s.shape, 1)
            s = jnp.where(ki <= qi, s, NEG)
        m_prev = m_ref[...]
        m_new = jnp.maximum(m_prev, jnp.max(s, axis=0, keepdims=True))
        alpha = jnp.exp2(m_prev - m_new)
        p = jnp.exp2(s - m_new).astype(BF16)
        vt1 = jnp.concatenate([vt_ref[ik], jnp.ones((ONES_ROWS, tq), BF16)], axis=0)
        acc_ref[...] = alpha * acc_ref[...] + jnp.dot(vt1, p, preferred_element_type=F32)
        m_ref[...] = m_new

    scores(0, s0_ref)

    def body(j, carry):
        ik = 2 * j
        scores(ik + 1, s1_ref)
        update(ik, s0_ref, False)
        scores(ik + 2, s0_ref)
        update(ik + 1, s1_ref, False)
        return carry

    lax.fori_loop(0, lax.div(iq, 2), body, 0)

    @pl.when(lax.rem(iq, 2) == 0)
    def _():
        update(iq, s0_ref, True)

    @pl.when(lax.rem(iq, 2) == 1)
    def _():
        scores(iq, s1_ref)
        update(iq - 1, s0_ref, False)
        update(iq, s1_ref, True)

    out = acc_ref[:V_HEAD, :] / acc_ref[V_HEAD:V_HEAD + 1, :]
    o_ref[...] = jnp.transpose(out).astype(o_ref.dtype)


N_SLOTS = 3


def _page_copies(pt_ref, lat_hbm, kr_hbm, lat_buf, kr_buf, sem, d, slot, pg, layer):
    cps = []
    for i in range(pg):
        page = pt_ref[d * pg + i]
        cps.append(pltpu.make_async_copy(lat_hbm.at[layer, page], lat_buf.at[slot, i], sem.at[0, slot]))
        cps.append(pltpu.make_async_copy(kr_hbm.at[layer, page], kr_buf.at[slot, i], sem.at[1, slot]))
    return cps


def _decode_group(q_ref, lat_buf, kr_buf, slot, m_ref, l_ref, acc_ref, pg, n_chains):
    q = q_ref[...]
    q_lat, q_pe = q[:, :KV_RANK], q[:, KV_RANK:]
    per = pg // n_chains
    lats, scores = [], []
    for c in range(n_chains):
        pages = range(c * per, (c + 1) * per)
        lat = jnp.concatenate([lat_buf[slot, i].astype(BF16) for i in pages], axis=0)
        krt = jnp.concatenate([kr_buf[slot, i] for i in pages], axis=1)
        krt = jnp.concatenate([krt, jnp.zeros_like(krt)], axis=0).astype(BF16)
        lats.append(lat)
        scores.append(_nt_dot(q_lat, lat) + jnp.dot(q_pe, krt, preferred_element_type=F32))
    for c in range(n_chains):
        s, m_prev = scores[c], m_ref[c]
        m_new = jnp.maximum(m_prev, jnp.max(s, axis=-1, keepdims=True))
        alpha = jnp.exp2(m_prev - m_new)
        p = jnp.exp2(s - m_new)
        l_ref[c] = alpha * l_ref[c] + jnp.sum(p, axis=-1, keepdims=True)
        acc_ref[c] = alpha * acc_ref[c] + jnp.dot(p.astype(BF16), lats[c], preferred_element_type=F32)
        m_ref[c] = m_new


def _decode_finish(q_ref, new_ref, o_ref, m_ref, l_ref, acc_ref, n_chains):
    m_run = m_ref[0]
    for c in range(1, n_chains):
        m_run = jnp.maximum(m_run, m_ref[c])
    l_run = jnp.zeros_like(m_run)
    acc = jnp.zeros(acc_ref.shape[1:], F32)
    for c in range(n_chains):
        w = jnp.exp2(m_ref[c] - m_run)
        l_run = l_run + w * l_ref[c]
        acc = acc + w * acc_ref[c]
    qf = q_ref[...].astype(F32)
    row = lax.broadcasted_iota(jnp.int32, m_run.shape, 0)
    for t in range(DEC_SEQ):
        kn = new_ref[t:t + 1, :]
        st = jnp.sum(qf * kn, axis=-1, keepdims=True)
        st = jnp.where(row >= t * MLA_HEADS, st, NEG)
        m_nxt = jnp.maximum(m_run, st)
        a = jnp.exp2(m_run - m_nxt)
        pt = jnp.exp2(st - m_nxt)
        l_run = a * l_run + pt
        acc = a * acc + pt * kn[:, :KV_RANK]
        m_run = m_nxt
    o_ref[...] = (acc / l_run).astype(o_ref.dtype)


def _query_block(i, nq):
    return jnp.where(lax.rem(i, 2) == 0, nq - 1 - lax.div(i, 2), lax.div(i, 2))


def _attention_kernel(pt_ref, q_ref, k_ref, vt_ref, qd_ref, new_ref, lat_hbm, kr_hbm, o_ref, od_ref,
                      s0_ref, s1_ref, m_ref, acc_ref, lat_buf, kr_buf, sem, md_ref, ld_ref, accd_ref,
                      *, tq, pg, n_chains, layer, n_steps):
    f = (pl.program_id(0) * pl.num_programs(1) + pl.program_id(1)) * pl.num_programs(2) + pl.program_id(2)
    d_a, d_b = 2 * f, 2 * f + 1

    def copies(d):
        return _page_copies(pt_ref, lat_hbm, kr_hbm, lat_buf, kr_buf, sem, d, lax.rem(d, N_SLOTS), pg, layer)

    @pl.when(f == 0)
    def _():
        for cp in copies(d_a):
            cp.start()

    for cp in copies(d_b):
        cp.start()

    _attn_block(_query_block(pl.program_id(2), pl.num_programs(2)), q_ref, k_ref, vt_ref, o_ref,
                s0_ref, s1_ref, m_ref, acc_ref, tq)

    @pl.when(f + 1 < n_steps)
    def _():
        for cp in copies(d_a + 2):
            cp.start()

    @pl.when(lax.rem(f, 2) == 0)
    def _():
        md_ref[...] = jnp.full_like(md_ref, NEG)
        ld_ref[...] = jnp.zeros_like(ld_ref)
        accd_ref[...] = jnp.zeros_like(accd_ref)

    for d in (d_a, d_b):
        for cp in copies(d):
            cp.wait()
        _decode_group(qd_ref, lat_buf, kr_buf, lax.rem(d, N_SLOTS), md_ref, ld_ref, accd_ref, pg, n_chains)

    @pl.when(lax.rem(f, 2) == 1)
    def _():
        _decode_finish(qd_ref, new_ref, od_ref, md_ref, ld_ref, accd_ref, n_chains)


def _attention(q, k, vt, q_abs, ckv_new, cache_latent, cache_krope_t, page_table, layer, *,
               batch, seq, heads, tq, out_rows, pg=16, n_chains=4):
    nq = seq // tq
    n_steps = batch * heads * nq
    rows = DEC_SEQ * MLA_HEADS
    assert N_PAGES == 4 * pg and 2 * n_steps == DEC_BATCH * (N_PAGES // pg), "two page groups per prompt grid step"

    def seq_of(b, h, i):
        return ((b * heads + h) * nq + i) // 2

    grid_spec = pltpu.PrefetchScalarGridSpec(
        num_scalar_prefetch=1,
        grid=(batch, heads, nq),
        in_specs=[pl.BlockSpec((tq, QH), lambda b, h, i, pt: (b * nq + _query_block(i, nq), h)),
                  pl.BlockSpec((seq, QH), lambda b, h, i, pt: (b, h)),
                  pl.BlockSpec((nq, V_HEAD, tq), lambda b, h, i, pt: (b, h, 0)),
                  pl.BlockSpec((None, rows, KVP), lambda b, h, i, pt: (seq_of(b, h, i), 0, 0)),
                  pl.BlockSpec((None, DEC_SEQ, KVP), lambda b, h, i, pt: (seq_of(b, h, i), 0, 0)),
                  pl.BlockSpec(memory_space=pl.ANY),
                  pl.BlockSpec(memory_space=pl.ANY)],
        out_specs=[pl.BlockSpec((tq, V_HEAD), lambda b, h, i, pt: (b * nq + _query_block(i, nq), h)),
                   pl.BlockSpec((None, rows, KV_RANK), lambda b, h, i, pt: (seq_of(b, h, i), 0, 0))],
        scratch_shapes=[pltpu.VMEM((tq, tq), F32), pltpu.VMEM((tq, tq), F32),
                        pltpu.VMEM((1, tq), F32), pltpu.VMEM((V_HEAD + ONES_ROWS, tq), F32),
                        pltpu.VMEM((N_SLOTS, pg, PAGE_SIZE, KV_RANK), F32),
                        pltpu.VMEM((N_SLOTS, pg, QK_ROPE, PAGE_SIZE), F32),
                        pltpu.SemaphoreType.DMA((2, N_SLOTS)),
                        pltpu.VMEM((n_chains, rows, 1), F32), pltpu.VMEM((n_chains, rows, 1), F32),
                        pltpu.VMEM((n_chains, rows, KV_RANK), F32)],
    )
    return pl.pallas_call(
        functools.partial(_attention_kernel, tq=tq, pg=pg, n_chains=n_chains, layer=layer, n_steps=n_steps),
        grid_spec=grid_spec,
        out_shape=[jax.ShapeDtypeStruct((out_rows, heads * V_HEAD), BF16),
                   jax.ShapeDtypeStruct((DEC_BATCH, rows, KV_RANK), BF16)],
        compiler_params=_params(("arbitrary", "arbitrary", "arbitrary")),
        name="attention",
    )(page_table.reshape(-1), q, k, vt, q_abs, ckv_new, cache_latent, cache_krope_t)


def _pool_prompt_kernel(u_ref, halo_ref, maps_ref, scale_ref, o_ref, *, tm):
    row0 = lax.rem(pl.program_id(0) * tm, SEQ)
    pos1 = row0 + 1 + lax.broadcasted_iota(jnp.int32, (tm, 1), 0)
    for g, w in enumerate(POOL_WINDOWS):
        sl = slice(g * POOL_GROUP, (g + 1) * POOL_GROUP)
        u = u_ref[:, sl]
        halo = jnp.where(row0 != 0, halo_ref[:, sl], 0.0)
        acc = jnp.concatenate([halo, u], axis=0)
        shift = 1
        while shift < w:
            acc = acc + pltpu.roll(acc, shift, 0)
            shift *= 2
        cnt = jnp.minimum(w, pos1).astype(F32)
        d = acc[POOL_CTX + 1:, :] / cnt - u
        y = jnp.dot(d.astype(BF16), maps_ref[g], preferred_element_type=F32) * scale_ref[:, sl]
        o_ref[:, sl] = y.astype(o_ref.dtype)


def _pool_prompt(u, maps, scale, tm=512):
    hb = POOL_CTX + 1
    return pl.pallas_call(
        functools.partial(_pool_prompt_kernel, tm=tm),
        grid=(NP // tm,),
        in_specs=[pl.BlockSpec((tm, POOL_W), lambda i: (i, 0)),
                  pl.BlockSpec((hb, POOL_W), lambda i: (jnp.maximum(i * (tm // hb) - 1, 0), 0)),
                  pl.BlockSpec(maps.shape, lambda i: (0, 0, 0)),
                  pl.BlockSpec((1, POOL_W), lambda i: (0, 0))],
        out_specs=pl.BlockSpec((tm, POOL_W), lambda i: (i, 0)),
        out_shape=jax.ShapeDtypeStruct((NT, POOL_W), BF16),
        compiler_params=_params(("parallel",)),
        name="pool_prompt",
    )(u, u, maps, scale)


def _pool_sample_kernel(ctx_ref, maps_ref, scale_ref, o_ref):
    for g, w in enumerate(POOL_WINDOWS):
        sl = slice(g * POOL_GROUP, (g + 1) * POOL_GROUP)
        rows = []
        for t in range(DEC_SEQ):
            r = POOL_CTX + t
            acc = ctx_ref[r, :, sl]
            for j in range(1, w):
                acc = acc + ctx_ref[r - j, :, sl]
            cnt = float(min(w, PAST_LEN - POOL_CTX + r + 1))
            rows.append(acc / cnt - ctx_ref[r, :, sl])
        d = jnp.concatenate(rows, axis=0).astype(BF16)
        y = jnp.dot(d, maps_ref[g], preferred_element_type=F32) * scale_ref[:, sl]
        bb = y.shape[0] // DEC_SEQ
        for t in range(DEC_SEQ):
            o_ref[t, :, sl] = y[t * bb:(t + 1) * bb].astype(o_ref.dtype)


def _pool_sample(ctx_t, maps, scale, bb=32):
    tc = POOL_CTX + DEC_SEQ
    return pl.pallas_call(
        _pool_sample_kernel,
        grid=(DEC_BATCH // bb,),
        in_specs=[pl.BlockSpec((tc, bb, POOL_W), lambda i: (0, i, 0)),
                  pl.BlockSpec(maps.shape, lambda i: (0, 0, 0)),
                  pl.BlockSpec((1, POOL_W), lambda i: (0, 0))],
        out_specs=pl.BlockSpec((DEC_SEQ, bb, POOL_W), lambda i: (0, i, 0)),
        out_shape=jax.ShapeDtypeStruct((DEC_SEQ, DEC_BATCH, POOL_W), BF16),
        compiler_params=_params(("parallel",)),
        name="pool_sample",
    )(ctx_t, maps, scale)


def _pad_rows(x, n):
    if x.shape[0] == n:
        return x
    return jnp.concatenate([x, jnp.zeros((n - x.shape[0], x.shape[1]), x.dtype)], axis=0)


def _mlstm_kernel(q_ref, k_ref, v_ref, og_ref, zt_ref, bias_ref, g_ref, c0_ref, n0_ref, m0_ref, *refs, L, rows):
    h_ref, c_ref, n_ref, m_ref = refs[-4:]

    @pl.when(pl.program_id(1) == 0)
    def _():
        c_ref[...] = c0_ref[...]
        n_ref[...] = n0_ref[...]
        m_ref[...] = m0_ref[...]

    zb = zt_ref[...] + bias_ref[...]
    li_all = zb[:MLSTM_HEADS]
    fx = zb[MLSTM_HEADS:]
    lf_all = jnp.minimum(fx, 0.0) - jnp.log1p(jnp.exp(-jnp.abs(fx)))
    ri = lax.broadcasted_iota(jnp.int32, (L, L), 0)
    ci = lax.broadcasted_iota(jnp.int32, (L, L), 1)
    eye = ri == ci
    tril = ci <= ri
    triu = ri <= ci

    for h in range(MLSTM_HEADS):
        hs = slice(h * MLSTM_DH, (h + 1) * MLSTM_DH)
        qc = _pad_rows(q_ref[:, hs], L)
        kc = _pad_rows(k_ref[:, hs], L) * jnp.asarray(MLSTM_DH ** -0.5, BF16)
        vc = _pad_rows(v_ref[:, hs], L)
        li_r = li_all[h:h + 1, :]
        lf_r = lf_all[h:h + 1, :]
        li_c = jnp.sum(jnp.where(eye, li_r, 0.0), axis=1, keepdims=True)
        lf_c = jnp.sum(jnp.where(eye, lf_r, 0.0), axis=1, keepdims=True)
        bcum_c = jnp.sum(jnp.where(tril, lf_r, 0.0), axis=1, keepdims=True)
        bcum_r = jnp.sum(jnp.where(triu, lf_c, 0.0), axis=0, keepdims=True)
        m_prev = m_ref[h:h + 1, :]
        c_prev = c_ref[h]
        n_prev = n_ref[h:h + 1, :]

        dmat = jnp.where(tril, bcum_c - bcum_r + li_r, NEG)
        m_t = jnp.maximum(bcum_c + m_prev, jnp.max(dmat, axis=1, keepdims=True))
        inter = jnp.exp(bcum_c + m_prev - m_t)
        wts = jnp.exp(dmat - m_t)
        s = _nt_dot(qc, kc) * wts
        num = inter * jnp.dot(qc, c_prev.astype(BF16), preferred_element_type=F32) \
            + jnp.dot(s.astype(BF16), vc, preferred_element_type=F32)
        qn = jnp.sum(qc.astype(F32) * n_prev, axis=1, keepdims=True)
        den = inter * qn + jnp.sum(s, axis=1, keepdims=True)
        hout = num / jnp.maximum(jnp.abs(den), jnp.exp(-m_t))
        hout = hout[:rows]
        ms = jnp.mean(hout * hout, axis=-1, keepdims=True)
        hn = hout * lax.rsqrt(ms + EPS) * g_ref[:, hs]
        gate = jax.nn.sigmoid(og_ref[:, hs].astype(F32))
        h_ref[:, hs] = (hn * gate).astype(h_ref.dtype)

        m_last = m_t[L - 1:L, :]
        b_last = bcum_c[L - 1:L, :]
        w_last = jnp.exp(b_last - bcum_c + li_c - m_last)
        decay = jnp.exp(b_last + m_prev - m_last)
        kw = kc.astype(F32) * w_last
        c_ref[h] = decay * c_prev + _tn_dot(kw.astype(BF16), vc)
        n_ref[h:h + 1, :] = decay * n_prev + jnp.sum(kw, axis=0, keepdims=True)
        m_ref[h:h + 1, :] = m_last


def _mlstm(mqkvo, zt, bias, gain, init, init_layer, prev, layer, *, nb, n_chunks, rows, L, out_rows):
    def tok_spec(col):
        return pl.BlockSpec((rows, MLSTM_W), lambda b, c: (b * n_chunks + c, col))

    def state_specs(li):
        return [pl.BlockSpec((None, None, MLSTM_HEADS, MLSTM_DH, MLSTM_DH), lambda b, c: (li, b, 0, 0, 0)),
                pl.BlockSpec((None, None, MLSTM_HEADS, MLSTM_DH), lambda b, c: (li, b, 0, 0)),
                pl.BlockSpec((None, None, MLSTM_HEADS, 1), lambda b, c: (li, b, 0, 0))]

    prev = tuple(prev) if prev is not None else ()
    n_in = 10
    return pl.pallas_call(
        functools.partial(_mlstm_kernel, L=L, rows=rows),
        grid=(nb, n_chunks),
        in_specs=[tok_spec(0), tok_spec(1), tok_spec(2), tok_spec(3),
                  pl.BlockSpec((None, 2 * MLSTM_HEADS, L), lambda b, c: (b, 0, c)),
                  pl.BlockSpec((2 * MLSTM_HEADS, 1), lambda b, c: (0, 0)),
                  pl.BlockSpec((1, MLSTM_W), lambda b, c: (0, 0))] + state_specs(init_layer)
        + [pl.BlockSpec(memory_space=pl.ANY)] * len(prev),
        out_specs=[pl.BlockSpec((rows, MLSTM_W), lambda b, c: (b * n_chunks + c, 0))] + state_specs(layer),
        out_shape=[jax.ShapeDtypeStruct((out_rows, MLSTM_W), BF16),
                   jax.ShapeDtypeStruct((DEPTH, nb, MLSTM_HEADS, MLSTM_DH, MLSTM_DH), F32),
                   jax.ShapeDtypeStruct((DEPTH, nb, MLSTM_HEADS, MLSTM_DH), F32),
                   jax.ShapeDtypeStruct((DEPTH, nb, MLSTM_HEADS, 1), F32)],
        input_output_aliases={n_in + i: 1 + i for i in range(len(prev))},
        compiler_params=_params(("parallel", "arbitrary")),
        name="mlstm",
    )(mqkvo, mqkvo, mqkvo, mqkvo, zt, bias, gain, *init, *prev)


def _merge_kernel(xa_ref, xb_ref, xc_ref, wa_ref, wb_ref, wc_ref, ga_ref, gb_ref, gc_ref, o_ref):
    out = None
    for x_ref, w_ref, g_ref in ((xa_ref, wa_ref, ga_ref), (xb_ref, wb_ref, gb_ref), (xc_ref, wc_ref, gc_ref)):
        br = jnp.dot(x_ref[...], w_ref[...], preferred_element_type=F32)
        term = jax.nn.sigmoid(g_ref[...].astype(F32)) * br
        out = term if out is None else out + term
    o_ref[...] = out.astype(o_ref.dtype)


def _merge(xs, ws, zg, layer, tm=1088, tn=512):
    nj = D_MODEL // tn
    assert NT % tm == 0 and D_MODEL % tn == 0
    x_spec = pl.BlockSpec((tm, D_MODEL), lambda i, j: (i, 0))
    w_spec = pl.BlockSpec((None, D_MODEL, tn), lambda i, j: (layer, 0, j))
    return pl.pallas_call(
        _merge_kernel,
        grid=(NT // tm, nj),
        in_specs=[x_spec] * N_BRANCH + [w_spec] * N_BRANCH
        + [pl.BlockSpec((tm, tn), functools.partial(lambda i, j, b: (i, b * nj + j), b=b)) for b in range(N_BRANCH)],
        out_specs=pl.BlockSpec((tm, tn), lambda i, j: (i, j)),
        out_shape=jax.ShapeDtypeStruct((NT, D_MODEL), BF16),
        compiler_params=_params(("parallel", "parallel")),
        name="branch_merge",
    )(*xs, *ws, zg, zg, zg)


def _add_residual(acc, r_ref):
    return acc + r_ref[...]


def _relu2(acc):
    return jnp.square(jnp.maximum(acc, 0.0))


def _layer(x, l, rope, mstate, cache_latent, cache_krope_t, page_table, state_C, state_n, state_m, state_pool,
           g_mix, w_in, kv_norm_g, w_ukv, w_pool, pool_scale, b_i, b_f, mlstm_norm_g,
           w_proj_attn, w_proj_pool, w_proj_mlstm, w_out, g_mlp, w_up, w_down):
    c_tab, s1_tab, s2_tab = rope
    tm = 1088
    rope_specs = [(t, pl.BlockSpec((tm, LANE), lambda i, j, k: (i, 0))) for t in (c_tab, s1_tab, s2_tab)]

    wl = w_in[l]
    o = np.cumsum((0, Q_COLS, KV_COLS, POOL_W, 4 * MLSTM_W, 2 * MLSTM_HEADS, N_BRANCH * D_MODEL)).tolist()
    wq = wl[:, o[0]:o[1]].reshape(D_MODEL, MLA_HEADS, QK_NOPE + QK_ROPE)
    wq = jnp.pad(wq, ((0, 0), (0, 0), (0, QH - QK_NOPE - QK_ROPE))).reshape(D_MODEL, MLA_HEADS * QH).astype(BF16)
    wkv = jnp.pad(wl[:, o[1]:o[2]], ((0, 0), (0, KVP - KV_COLS))).astype(BF16)
    wu = wl[:, o[2]:o[3]].astype(BF16)
    wm = wl[:, o[3]:o[4]].astype(BF16)
    wif = jnp.pad(wl[:, o[4]:o[5]], ((0, 0), (0, LANE - 2 * MLSTM_HEADS))).astype(BF16)
    wg = wl[:, o[5]:o[6]].astype(BF16)

    w_uk = w_ukv[l][..., :QK_NOPE]
    w_uv = w_ukv[l][..., QK_NOPE:]
    eye_r = jnp.eye(QK_ROPE, dtype=F32)
    k_top = jnp.pad(w_uk, ((0, 0), (0, 0), (0, QH - QK_NOPE)))
    k_bot = jnp.broadcast_to(jnp.pad(eye_r, ((0, LANE - QK_ROPE), (QK_NOPE, QH - QK_NOPE - QK_ROPE)))[:, None, :],
                             (LANE, MLA_HEADS, QH))
    w_kb = jnp.concatenate([k_top, k_bot], axis=0).reshape(KVP, MLA_HEADS * QH).astype(BF16)
    w_vt = jnp.pad(jnp.transpose(w_uv, (1, 2, 0)).reshape(MLA_HEADS * V_HEAD, KV_RANK),
                   ((0, 0), (0, KVP - KV_RANK))).astype(BF16)
    a_top = jnp.pad(jnp.transpose(w_uk, (1, 2, 0)), ((0, 0), (0, 0), (0, KVP - KV_RANK)))
    a_bot = jnp.broadcast_to(jnp.pad(eye_r, ((0, QH - QK_NOPE - QK_ROPE), (KV_RANK, KVP - KV_RANK - QK_ROPE)))[None],
                             (MLA_HEADS, LANE, KVP))
    w_abs = jnp.concatenate([a_top, a_bot], axis=1).astype(BF16)
    w_uvh = jnp.transpose(w_uv, (1, 0, 2)).astype(BF16)

    h = _rmsnorm(x, g_mix[l], BF16)

    q = _matmul(h, wq, tm=tm, tn=4 * QH, out_dtype=BF16, name="q_proj", epilogue=_q_epilogue, extras=rope_specs)
    ckv = _matmul(h, wkv, tm=tm, tn=KVP, out_dtype=F32, name="kv_proj", epilogue=_kv_epilogue,
                  extras=[(kv_norm_g[l].reshape(1, KV_RANK), pl.BlockSpec((1, KV_RANK), lambda i, j, k: (0, 0)))]
                  + rope_specs)
    tq = 512
    k_p = _matmul(ckv, w_kb, rows=NP, tm=1024, tn=1024, out_dtype=BF16, name="k_up")
    vt_p = _vt_matmul(w_vt, ckv, rows=NP, tt=tq)
    q_abs = _blockdiag_matmul(q[NP:], w_abs, BF16, "q_absorb")
    o_p, o_lat = _attention(q, k_p, vt_p, q_abs.reshape(DEC_BATCH, DEC_SEQ * MLA_HEADS, KVP),
                            ckv[NP:].reshape(DEC_BATCH, DEC_SEQ, KVP), cache_latent, cache_krope_t, page_table, l,
                            batch=BATCH, seq=SEQ, heads=MLA_HEADS, tq=tq, out_rows=NT)
    o_s = _blockdiag_matmul(o_lat.reshape(NS, MLA_HEADS * KV_RANK), w_uvh, BF16, "v_up_sample")
    o_attn = o_p.at[NP:].set(o_s)

    u = _matmul(h, wu, tm=2 * tm, tn=512, out_dtype=F32, name="pool_in")
    maps = w_pool[l].astype(BF16)
    scale = pool_scale[l].reshape(1, POOL_W)
    u_s = u[NP:].reshape(DEC_BATCH, DEC_SEQ, POOL_W)
    ctx_s = jnp.concatenate([state_pool[l], u_s], axis=1)
    pm_s = _pool_sample(jnp.transpose(ctx_s, (1, 0, 2)), maps, scale)
    pm = _pool_prompt(u, maps, scale).at[NP:].set(jnp.transpose(pm_s, (1, 0, 2)).reshape(NS, POOL_W))

    mz = _matmul(h, wm, tm=2 * tm, tn=1024, out_dtype=BF16, name="mlstm_in")
    zif = _matmul(h, wif, tm=tm, tn=LANE, out_dtype=F32, name="gate_in")[:, :2 * MLSTM_HEADS]
    bias = jnp.concatenate([b_i[l], b_f[l]]).reshape(2 * MLSTM_HEADS, 1)
    gain = mlstm_norm_g[l].reshape(1, MLSTM_W)
    lp, ls, rs = 256, LANE, 16
    zt_p = jnp.transpose(zif[:NP].reshape(BATCH, SEQ, 2 * MLSTM_HEADS), (0, 2, 1))
    zero_c = jnp.zeros((1, BATCH, MLSTM_HEADS, MLSTM_DH, MLSTM_DH), F32)
    p_prev, s_prev = mstate if mstate is not None else (None, None)
    hm_p, *p_new = _mlstm(mz, zt_p, bias, gain, (zero_c, zero_c[..., 0], zero_c[..., :1, 0]), 0, p_prev, l,
                          nb=BATCH, n_chunks=SEQ // lp, rows=lp, L=lp, out_rows=NT)
    mz_s = jnp.pad(mz[NP:].reshape(DEC_BATCH, DEC_SEQ, 4 * MLSTM_W), ((0, 0), (0, rs - DEC_SEQ), (0, 0)))
    zt_s = jnp.transpose(zif[NP:].reshape(DEC_BATCH, DEC_SEQ, 2 * MLSTM_HEADS), (0, 2, 1))
    pad_i = jnp.full((DEC_BATCH, MLSTM_HEADS, ls - DEC_SEQ), NEG, F32)
    zt_s = jnp.concatenate([zt_s, jnp.concatenate([pad_i, -pad_i], axis=1)], axis=2)
    hm_s, *s_new = _mlstm(mz_s.reshape(DEC_BATCH * rs, 4 * MLSTM_W), zt_s, bias, gain,
                          (state_C, state_n, state_m[..., None]), l, s_prev, l,
                          nb=DEC_BATCH, n_chunks=1, rows=rs, L=ls, out_rows=DEC_BATCH * rs)
    hm_s = hm_s.reshape(DEC_BATCH, rs, MLSTM_W)[:, :DEC_SEQ].reshape(NS, MLSTM_W)
    hm = hm_p.at[NP:].set(hm_s)

    zg = _matmul(h, wg, tm=2 * tm, tn=1024, out_dtype=BF16, name="branch_gates")
    merged = _merge((o_attn, pm, hm), [w.astype(BF16) for w in (w_proj_attn, w_proj_pool, w_proj_mlstm)], zg, l)
    x = _matmul(merged, w_out.astype(BF16), layer=l, tm=tm, tn=1024, out_dtype=F32, name="out_proj",
                epilogue=_add_residual, extras=[(x, pl.BlockSpec((tm, 1024), lambda i, j, k: (i, j)))])

    hf = _rmsnorm(x, g_mlp[l], BF16)
    a = _matmul(hf, w_up.astype(BF16), layer=l, tm=2 * tm, tn=1024, out_dtype=BF16, name="mlp_up", epilogue=_relu2)
    x = _matmul(a, w_down.astype(BF16), layer=l, tm=tm // 2, tn=512, out_dtype=F32, name="mlp_down",
                epilogue=_add_residual, extras=[(x, pl.BlockSpec((tm // 2, 512), lambda i, j, k: (i, j)))])

    p_state = (ckv[:NP, :KV_RANK].reshape(BATCH, SEQ, KV_RANK),
               ckv[:NP, KV_RANK:KV_COLS].reshape(BATCH, SEQ, QK_ROPE),
               jnp.stack([u[(b + 1) * SEQ - POOL_CTX:(b + 1) * SEQ] for b in range(BATCH)]))
    s_state = (ckv[NP:, :KV_RANK].reshape(DEC_BATCH, DEC_SEQ, KV_RANK),
               ckv[NP:, KV_RANK:KV_COLS].reshape(DEC_BATCH, DEC_SEQ, QK_ROPE),
               ctx_s[:, -POOL_CTX:])
    return x, p_state, s_state, (p_new, s_new)


def kernel(x_prompt, x_sample, cache_latent, cache_krope, page_table, state_C, state_n, state_m, state_pool,
           g_mix, w_in, kv_norm_g, w_ukv, w_pool, pool_scale, b_i, b_f, mlstm_norm_g,
           w_proj_attn, w_proj_pool, w_proj_mlstm, w_out, g_mlp, w_up, w_down, g_final):
    x = jnp.concatenate([x_prompt.reshape(NP, D_MODEL), x_sample.reshape(NS, D_MODEL)], axis=0).astype(F32)
    rope = _rope_tables()
    cache_krope_t = jnp.swapaxes(cache_krope, 2, 3)
    p_states, s_states, mstate = [], [], None
    for l in range(DEPTH):
        x, ps, ss, mstate = _layer(x, l, rope, mstate, cache_latent, cache_krope_t, page_table, state_C, state_n,
                                   state_m, state_pool, g_mix, w_in, kv_norm_g, w_ukv, w_pool, pool_scale, b_i, b_f,
                                   mlstm_norm_g, w_proj_attn, w_proj_pool, w_proj_mlstm, w_out, g_mlp, w_up, w_down)
        p_states.append(ps)
        s_states.append(ss)
    y_prompt = _rmsnorm(x, g_final, F32, tm=512, rows=NP).reshape(BATCH, SEQ, D_MODEL).astype(x_prompt.dtype)
    y_sample = _rmsnorm(x[NP:], g_final, F32, tm=NS).reshape(DEC_BATCH, DEC_SEQ, D_MODEL).astype(x_sample.dtype)

    def stack(states, i, like):
        return jnp.stack([s[i] for s in states]).astype(like.dtype)

    (p_c, p_n, p_m), (s_c, s_n, s_m) = mstate
    return (y_prompt, y_sample,
            stack(p_states, 0, cache_latent), stack(p_states, 1, cache_krope),
            p_c.astype(state_C.dtype), p_n.astype(state_n.dtype), p_m[..., 0].astype(state_m.dtype),
            stack(p_states, 2, state_pool),
            stack(s_states, 0, cache_latent), stack(s_states, 1, cache_krope),
            s_c.astype(state_C.dtype), s_n.astype(state_n.dtype), s_m[..., 0].astype(state_m.dtype),
            stack(s_states, 2, state_pool))
```

```python
import functools

import numpy as np
import jax
import jax.numpy as jnp
from jax import lax
from jax.experimental import pallas as pl
from jax.experimental.pallas import tpu as pltpu

D_MODEL = 2048
BATCH = 2
SEQ = 4096
DEPTH = 2
DEC_BATCH = 128
DEC_SEQ = 4
PAST_LEN = 8192
PAGE_SIZE = 128
N_PAGES = PAST_LEN // PAGE_SIZE

MLA_HEADS = 16
QK_NOPE = 128
QK_ROPE = 64
V_HEAD = 128
KV_RANK = 512
ROPE_THETA = 10000.0
ATTN_SCALE = (QK_NOPE + QK_ROPE) ** -0.5
LOG2E = 1.4426950408889634
Q_COLS = MLA_HEADS * (QK_NOPE + QK_ROPE)
KV_COLS = KV_RANK + QK_ROPE

POOL_WINDOWS = (2, 4, 8, 16)
POOL_W = 2048
POOL_GROUP = POOL_W // len(POOL_WINDOWS)
POOL_CTX = max(POOL_WINDOWS) - 1

MLSTM_HEADS = 8
MLSTM_DH = 256
MLSTM_W = MLSTM_HEADS * MLSTM_DH
N_BRANCH = 3
D_FF = 4 * D_MODEL
EPS = 1e-6

NP = BATCH * SEQ
NS = DEC_BATCH * DEC_SEQ
NT = NP + NS

LANE = 128
QH = 2 * LANE
KVP = KV_RANK + LANE
NEG = -1e30
ONES_ROWS = 16
VMEM_LIMIT = 56 * 1024 * 1024

F32 = jnp.float32
BF16 = jnp.bfloat16


def _params(sem, vmem=VMEM_LIMIT):
    return pltpu.CompilerParams(dimension_semantics=sem, vmem_limit_bytes=vmem)


def _nt_dot(a, b):
    return lax.dot_general(a, b, (((1,), (1,)), ((), ())), preferred_element_type=F32)


def _tn_dot(a, b):
    return lax.dot_general(a, b, (((0,), (0,)), ((), ())), preferred_element_type=F32)


def _rmsnorm_kernel(x_ref, g_ref, o_ref):
    x = x_ref[...]
    ms = jnp.mean(x * x, axis=-1, keepdims=True)
    o_ref[...] = (x * lax.rsqrt(ms + EPS) * g_ref[...]).astype(o_ref.dtype)


def _rmsnorm(x, g, out_dtype, tm=544, rows=None):
    m = rows or x.shape[0]
    d = x.shape[1]
    assert m % tm == 0, (m, tm)
    return pl.pallas_call(
        _rmsnorm_kernel,
        grid=(m // tm,),
        in_specs=[pl.BlockSpec((tm, d), lambda i: (i, 0)), pl.BlockSpec((1, d), lambda i: (0, 0))],
        out_specs=pl.BlockSpec((tm, d), lambda i: (i, 0)),
        out_shape=jax.ShapeDtypeStruct((m, d), out_dtype),
        compiler_params=_params(("parallel",)),
        name="rmsnorm",
    )(x, g.reshape(1, d))


def _matmul_kernel(*refs, n_extra, nk, epilogue):
    x_ref, w_ref = refs[0], refs[1]
    extra = refs[2:2 + n_extra]
    o_ref = refs[2 + n_extra]
    part = jnp.dot(x_ref[...].astype(BF16), w_ref[...], preferred_element_type=F32)
    if nk == 1:
        o_ref[...] = epilogue(part, *extra).astype(o_ref.dtype)
        return
    acc_ref = refs[3 + n_extra]
    k = pl.program_id(2)

    @pl.when(k == 0)
    def _():
        acc_ref[...] = part

    @pl.when(k > 0)
    def _():
        acc_ref[...] += part

    @pl.when(k == nk - 1)
    def _():
        o_ref[...] = epilogue(acc_ref[...], *extra).astype(o_ref.dtype)


def _matmul(x, w, *, tm, tn, out_dtype, name, tk=None, rows=None, layer=None, epilogue=None, extras=()):
    m = rows or x.shape[0]
    kdim, n = w.shape[-2:]
    tk = tk or kdim
    nk = kdim // tk
    assert m % tm == 0 and n % tn == 0 and kdim % tk == 0, (m, n, kdim, tm, tn, tk)
    if epilogue is None:
        epilogue = lambda acc: acc
    if layer is None:
        w_spec = pl.BlockSpec((tk, tn), lambda i, j, k: (k, j))
    else:
        w_spec = pl.BlockSpec((None, tk, tn), lambda i, j, k: (layer, k, j))
    kern = functools.partial(_matmul_kernel, n_extra=len(extras), nk=nk, epilogue=epilogue)
    return pl.pallas_call(
        kern,
        grid=(m // tm, n // tn, nk),
        in_specs=[pl.BlockSpec((tm, tk), lambda i, j, k: (i, k)), w_spec] + [s for _, s in extras],
        out_specs=pl.BlockSpec((tm, tn), lambda i, j, k: (i, j)),
        out_shape=jax.ShapeDtypeStruct((m, n), out_dtype),
        scratch_shapes=[pltpu.VMEM((tm, tn), F32)] if nk > 1 else [],
        compiler_params=_params(("parallel", "parallel", "arbitrary")),
        name=name,
    )(x, w, *[a for a, _ in extras])


def _blockdiag_kernel(x_ref, w_ref, o_ref):
    o_ref[...] = jnp.dot(x_ref[...], w_ref[...], preferred_element_type=F32).astype(o_ref.dtype)


def _blockdiag_matmul(x, w, out_dtype, name):
    m = x.shape[0]
    g, kg, ng = w.shape
    return pl.pallas_call(
        _blockdiag_kernel,
        grid=(g,),
        in_specs=[pl.BlockSpec((m, kg), lambda i: (0, i)), pl.BlockSpec((None, kg, ng), lambda i: (i, 0, 0))],
        out_specs=pl.BlockSpec((m, ng), lambda i: (0, i)),
        out_shape=jax.ShapeDtypeStruct((m, g * ng), out_dtype),
        compiler_params=_params(("parallel",)),
        name=name,
    )(x, w)


def _rope_lanes(pe, c_ref, s1_ref, s2_ref):
    return (pe * c_ref[...] + pltpu.roll(pe, LANE - QK_ROPE // 2, 1) * s1_ref[...]
            + pltpu.roll(pe, QK_ROPE // 2, 1) * s2_ref[...])


def _rope_tables():
    pos = jnp.concatenate([jnp.tile(jnp.arange(SEQ, dtype=F32), BATCH),
                           jnp.tile(PAST_LEN + jnp.arange(DEC_SEQ, dtype=F32), DEC_BATCH)])
    inv = ROPE_THETA ** (-jnp.arange(0, QK_ROPE, 2, dtype=F32) / QK_ROPE)
    ang = pos[:, None] * inv[None, :]
    cos, sin = jnp.cos(ang), jnp.sin(ang)
    z = jnp.zeros_like(cos)
    c = jnp.concatenate([cos, cos, z, z], axis=1)
    s1 = jnp.concatenate([-sin, z, z, z], axis=1)
    s2 = jnp.concatenate([z, sin, z, z], axis=1)
    return c, s1, s2


def _q_epilogue(acc, c_ref, s1_ref, s2_ref):
    parts = []
    for h in range(acc.shape[1] // QH):
        parts.append(acc[:, h * QH:h * QH + LANE])
        parts.append(_rope_lanes(acc[:, h * QH + LANE:(h + 1) * QH], c_ref, s1_ref, s2_ref))
    return jnp.concatenate(parts, axis=1) * (ATTN_SCALE * LOG2E)


def _kv_epilogue(acc, g_ref, c_ref, s1_ref, s2_ref):
    lat = acc[:, :KV_RANK]
    ms = jnp.mean(lat * lat, axis=-1, keepdims=True)
    lat = lat * lax.rsqrt(ms + EPS) * g_ref[...]
    return jnp.concatenate([lat, _rope_lanes(acc[:, KV_RANK:], c_ref, s1_ref, s2_ref)], axis=1)


def _vt_kernel(w_ref, x_ref, o_ref):
    o_ref[...] = _nt_dot(w_ref[...], x_ref[...].astype(BF16)).astype(o_ref.dtype)


def _vt_matmul(w_t, x, *, rows, tt, tn=2048):
    n, kdim = w_t.shape
    return pl.pallas_call(
        _vt_kernel,
        grid=(rows // tt, n // tn),
        in_specs=[pl.BlockSpec((tn, kdim), lambda j, i: (i, 0)), pl.BlockSpec((tt, kdim), lambda j, i: (j, 0))],
        out_specs=pl.BlockSpec((None, tn, tt), lambda j, i: (j, i, 0)),
        out_shape=jax.ShapeDtypeStruct((rows // tt, n, tt), BF16),
        compiler_params=_params(("parallel", "parallel")),
        name="v_up_t",
    )(w_t, x)


def _attn_block(iq, q_ref, k_ref, vt_ref, o_ref, s0_ref, s1_ref, m_ref, acc_ref, tq):
    q = q_ref[...]
    m_ref[...] = jnp.full_like(m_ref, NEG)
    acc_ref[...] = jnp.zeros_like(acc_ref)

    def scores(ik, s_ref):
        start = pl.multiple_of(ik * tq, tq)
        s_ref[...] = _nt_dot(k_ref[pl.ds(start, tq), :], q)

    def update(ik, s_ref, diagonal):
        s = s_ref[...]
        if diagonal:
            ki = lax.broadcasted_iota(jnp.int32, s.shape, 0)
            qi = lax.broadcasted_iota(jnp.int32, s.shape, 1)
            s = jnp.where(ki <= qi, s, NEG)
        m_prev = m_ref[...]
        m_new = jnp.maximum(m_prev, jnp.max(s, axis=0, keepdims=True))
        alpha = jnp.exp2(m_prev - m_new)
        p = jnp.exp2(s - m_new).astype(BF16)
        vt1 = jnp.concatenate([vt_ref[ik], jnp.ones((ONES_ROWS, tq), BF16)], axis=0)
        acc_ref[...] = alpha * acc_ref[...] + jnp.dot(vt1, p, preferred_element_type=F32)
        m_ref[...] = m_new

    scores(0, s0_ref)

    def body(j, carry):
        ik = 2 * j
        scores(ik + 1, s1_ref)
        update(ik, s0_ref, False)
        scores(ik + 2, s0_ref)
        update(ik + 1, s1_ref, False)
        return carry

    lax.fori_loop(0, lax.div(iq, 2), body, 0)

    @pl.when(lax.rem(iq, 2) == 0)
    def _():
        update(iq, s0_ref, True)

    @pl.when(lax.rem(iq, 2) == 1)
    def _():
        scores(iq, s1_ref)
        update(iq - 1, s0_ref, False)
        update(iq, s1_ref, True)

    out = acc_ref[:V_HEAD, :] / acc_ref[V_HEAD:V_HEAD + 1, :]
    o_ref[...] = jnp.transpose(out).astype(o_ref.dtype)


N_SLOTS = 3


def _page_copies(pt_ref, lat_hbm, kr_hbm, lat_buf, kr_buf, sem, d, slot, pg, layer):
    cps = []
    for i in range(pg):
        page = pt_ref[d * pg + i]
        cps.append(pltpu.make_async_copy(lat_hbm.at[layer, page], lat_buf.at[slot, i], sem.at[0, slot]))
        cps.append(pltpu.make_async_copy(kr_hbm.at[layer, page], kr_buf.at[slot, i], sem.at[1, slot]))
    return cps


def _decode_group(q_ref, lat_buf, kr_buf, slot, m_ref, l_ref, acc_ref, pg, n_chains):
    q = q_ref[...]
    q_lat, q_pe = q[:, :KV_RANK], q[:, KV_RANK:]
    per = pg // n_chains
    lats, scores = [], []
    for c in range(n_chains):
        pages = range(c * per, (c + 1) * per)
        lat = jnp.concatenate([lat_buf[slot, i].astype(BF16) for i in pages], axis=0)
        krt = jnp.concatenate([kr_buf[slot, i] for i in pages], axis=1)
        krt = jnp.concatenate([krt, jnp.zeros_like(krt)], axis=0).astype(BF16)
        lats.append(lat)
        scores.append(_nt_dot(q_lat, lat) + jnp.dot(q_pe, krt, preferred_element_type=F32))
    for c in range(n_chains):
        s, m_prev = scores[c], m_ref[c]
        m_new = jnp.maximum(m_prev, jnp.max(s, axis=-1, keepdims=True))
        alpha = jnp.exp2(m_prev - m_new)
        p = jnp.exp2(s - m_new)
        l_ref[c] = alpha * l_ref[c] + jnp.sum(p, axis=-1, keepdims=True)
        acc_ref[c] = alpha * acc_ref[c] + jnp.dot(p.astype(BF16), lats[c], preferred_element_type=F32)
        m_ref[c] = m_new


def _decode_finish(q_ref, new_ref, o_ref, m_ref, l_ref, acc_ref, n_chains):
    m_run = m_ref[0]
    for c in range(1, n_chains):
        m_run = jnp.maximum(m_run, m_ref[c])
    l_run = jnp.zeros_like(m_run)
    acc = jnp.zeros(acc_ref.shape[1:], F32)
    for c in range(n_chains):
        w = jnp.exp2(m_ref[c] - m_run)
        l_run = l_run + w * l_ref[c]
        acc = acc + w * acc_ref[c]
    qf = q_ref[...].astype(F32)
    row = lax.broadcasted_iota(jnp.int32, m_run.shape, 0)
    for t in range(DEC_SEQ):
        kn = new_ref[t:t + 1, :]
        st = jnp.sum(qf * kn, axis=-1, keepdims=True)
        st = jnp.where(row >= t * MLA_HEADS, st, NEG)
        m_nxt = jnp.maximum(m_run, st)
        a = jnp.exp2(m_run - m_nxt)
        pt = jnp.exp2(st - m_nxt)
        l_run = a * l_run + pt
        acc = a * acc + pt * kn[:, :KV_RANK]
        m_run = m_nxt
    o_ref[...] = (acc / l_run).astype(o_ref.dtype)


def _query_block(i, nq):
    return jnp.where(lax.rem(i, 2) == 0, nq - 1 - lax.div(i, 2), lax.div(i, 2))


def _attention_kernel(pt_ref, q_ref, k_ref, vt_ref, qd_ref, new_ref, lat_hbm, kr_hbm, o_ref, od_ref,
                      s0_ref, s1_ref, m_ref, acc_ref, lat_buf, kr_buf, sem, md_ref, ld_ref, accd_ref,
                      *, tq, pg, n_chains, layer, n_steps):
    f = (pl.program_id(0) * pl.num_programs(1) + pl.program_id(1)) * pl.num_programs(2) + pl.program_id(2)
    d_a, d_b = 2 * f, 2 * f + 1

    def copies(d):
        return _page_copies(pt_ref, lat_hbm, kr_hbm, lat_buf, kr_buf, sem, d, lax.rem(d, N_SLOTS), pg, layer)

    @pl.when(f == 0)
    def _():
        for cp in copies(d_a):
            cp.start()

    for cp in copies(d_b):
        cp.start()

    _attn_block(_query_block(pl.program_id(2), pl.num_programs(2)), q_ref, k_ref, vt_ref, o_ref,
                s0_ref, s1_ref, m_ref, acc_ref, tq)

    @pl.when(f + 1 < n_steps)
    def _():
        for cp in copies(d_a + 2):
            cp.start()

    @pl.when(lax.rem(f, 2) == 0)
    def _():
        md_ref[...] = jnp.full_like(md_ref, NEG)
        ld_ref[...] = jnp.zeros_like(ld_ref)
        accd_ref[...] = jnp.zeros_like(accd_ref)

    for d in (d_a, d_b):
        for cp in copies(d):
            cp.wait()
        _decode_group(qd_ref, lat_buf, kr_buf, lax.rem(d, N_SLOTS), md_ref, ld_ref, accd_ref, pg, n_chains)

    @pl.when(lax.rem(f, 2) == 1)
    def _():
        _decode_finish(qd_ref, new_ref, od_ref, md_ref, ld_ref, accd_ref, n_chains)


def _attention(q, k, vt, q_abs, ckv_new, cache_latent, cache_krope_t, page_table, layer, *,
               batch, seq, heads, tq, out_rows, pg=16, n_chains=4):
    nq = seq // tq
    n_steps = batch * heads * nq
    rows = DEC_SEQ * MLA_HEADS
    assert N_PAGES == 4 * pg and 2 * n_steps == DEC_BATCH * (N_PAGES // pg), "two page groups per prompt grid step"

    def seq_of(b, h, i):
        return ((b * heads + h) * nq + i) // 2

    grid_spec = pltpu.PrefetchScalarGridSpec(
        num_scalar_prefetch=1,
        grid=(batch, heads, nq),
        in_specs=[pl.BlockSpec((tq, QH), lambda b, h, i, pt: (b * nq + _query_block(i, nq), h)),
                  pl.BlockSpec((seq, QH), lambda b, h, i, pt: (b, h)),
                  pl.BlockSpec((nq, V_HEAD, tq), lambda b, h, i, pt: (b, h, 0)),
                  pl.BlockSpec((None, rows, KVP), lambda b, h, i, pt: (seq_of(b, h, i), 0, 0)),
                  pl.BlockSpec((None, DEC_SEQ, KVP), lambda b, h, i, pt: (seq_of(b, h, i), 0, 0)),
                  pl.BlockSpec(memory_space=pl.ANY),
                  pl.BlockSpec(memory_space=pl.ANY)],
        out_specs=[pl.BlockSpec((tq, V_HEAD), lambda b, h, i, pt: (b * nq + _query_block(i, nq), h)),
                   pl.BlockSpec((None, rows, KV_RANK), lambda b, h, i, pt: (seq_of(b, h, i), 0, 0))],
        scratch_shapes=[pltpu.VMEM((tq, tq), F32), pltpu.VMEM((tq, tq), F32),
                        pltpu.VMEM((1, tq), F32), pltpu.VMEM((V_HEAD + ONES_ROWS, tq), F32),
                        pltpu.VMEM((N_SLOTS, pg, PAGE_SIZE, KV_RANK), F32),
                        pltpu.VMEM((N_SLOTS, pg, QK_ROPE, PAGE_SIZE), F32),
                        pltpu.SemaphoreType.DMA((2, N_SLOTS)),
                        pltpu.VMEM((n_chains, rows, 1), F32), pltpu.VMEM((n_chains, rows, 1), F32),
                        pltpu.VMEM((n_chains, rows, KV_RANK), F32)],
    )
    return pl.pallas_call(
        functools.partial(_attention_kernel, tq=tq, pg=pg, n_chains=n_chains, layer=layer, n_steps=n_steps),
        grid_spec=grid_spec,
        out_shape=[jax.ShapeDtypeStruct((out_rows, heads * V_HEAD), BF16),
                   jax.ShapeDtypeStruct((DEC_BATCH, rows, KV_RANK), BF16)],
        compiler_params=_params(("arbitrary", "arbitrary", "arbitrary")),
        name="attention",
    )(page_table.reshape(-1), q, k, vt, q_abs, ckv_new, cache_latent, cache_krope_t)


def _pool_prompt_kernel(u_ref, halo_ref, maps_ref, scale_ref, o_ref, *, tm):
    row0 = lax.rem(pl.program_id(0) * tm, SEQ)
    pos1 = row0 + 1 + lax.broadcasted_iota(jnp.int32, (tm, 1), 0)
    for g, w in enumerate(POOL_WINDOWS):
        sl = slice(g * POOL_GROUP, (g + 1) * POOL_GROUP)
        u = u_ref[:, sl]
        halo = jnp.where(row0 != 0, halo_ref[:, sl], 0.0)
        acc = jnp.concatenate([halo, u], axis=0)
        shift = 1
        while shift < w:
            acc = acc + pltpu.roll(acc, shift, 0)
            shift *= 2
        cnt = jnp.minimum(w, pos1).astype(F32)
        d = acc[POOL_CTX + 1:, :] / cnt - u
        y = jnp.dot(d.astype(BF16), maps_ref[g], preferred_element_type=F32) * scale_ref[:, sl]
        o_ref[:, sl] = y.astype(o_ref.dtype)


def _pool_prompt(u, maps, scale, tm=512):
    hb = POOL_CTX + 1
    return pl.pallas_call(
        functools.partial(_pool_prompt_kernel, tm=tm),
        grid=(NP // tm,),
        in_specs=[pl.BlockSpec((tm, POOL_W), lambda i: (i, 0)),
                  pl.BlockSpec((hb, POOL_W), lambda i: (jnp.maximum(i * (tm // hb) - 1, 0), 0)),
                  pl.BlockSpec(maps.shape, lambda i: (0, 0, 0)),
                  pl.BlockSpec((1, POOL_W), lambda i: (0, 0))],
        out_specs=pl.BlockSpec((tm, POOL_W), lambda i: (i, 0)),
        out_shape=jax.ShapeDtypeStruct((NT, POOL_W), BF16),
        compiler_params=_params(("parallel",)),
        name="pool_prompt",
    )(u, u, maps, scale)


def _pool_sample_kernel(ctx_ref, maps_ref, scale_ref, o_ref):
    for g, w in enumerate(POOL_WINDOWS):
        sl = slice(g * POOL_GROUP, (g + 1) * POOL_GROUP)
        rows = []
        for t in range(DEC_SEQ):
            r = POOL_CTX + t
            acc = ctx_ref[r, :, sl]
            for j in range(1, w):
                acc = acc + ctx_ref[r - j, :, sl]
            cnt = float(min(w, PAST_LEN - POOL_CTX + r + 1))
            rows.append(acc / cnt - ctx_ref[r, :, sl])
        d = jnp.concatenate(rows, axis=0).astype(BF16)
        y = jnp.dot(d, maps_ref[g], preferred_element_type=F32) * scale_ref[:, sl]
        bb = y.shape[0] // DEC_SEQ
        for t in range(DEC_SEQ):
            o_ref[t, :, sl] = y[t * bb:(t + 1) * bb].astype(o_ref.dtype)


def _pool_sample(ctx_t, maps, scale, bb=32):
    tc = POOL_CTX + DEC_SEQ
    return pl.pallas_call(
        _pool_sample_kernel,
        grid=(DEC_BATCH // bb,),
        in_specs=[pl.BlockSpec((tc, bb, POOL_W), lambda i: (0, i, 0)),
                  pl.BlockSpec(maps.shape, lambda i: (0, 0, 0)),
                  pl.BlockSpec((1, POOL_W), lambda i: (0, 0))],
        out_specs=pl.BlockSpec((DEC_SEQ, bb, POOL_W), lambda i: (0, i, 0)),
        out_shape=jax.ShapeDtypeStruct((DEC_SEQ, DEC_BATCH, POOL_W), BF16),
        compiler_params=_params(("parallel",)),
        name="pool_sample",
    )(ctx_t, maps, scale)


def _pad_rows(x, n):
    if x.shape[0] == n:
        return x
    return jnp.concatenate([x, jnp.zeros((n - x.shape[0], x.shape[1]), x.dtype)], axis=0)


def _mlstm_kernel(q_ref, k_ref, v_ref, og_ref, zt_ref, bias_ref, g_ref, c0_ref, n0_ref, m0_ref, *refs, L, rows):
    h_ref, c_ref, n_ref, m_ref = refs[-4:]

    @pl.when(pl.program_id(1) == 0)
    def _():
        c_ref[...] = c0_ref[...]
        n_ref[...] = n0_ref[...]
        m_ref[...] = m0_ref[...]

    zb = zt_ref[...] + bias_ref[...]
    li_all = zb[:MLSTM_HEADS]
    fx = zb[MLSTM_HEADS:]
    lf_all = jnp.minimum(fx, 0.0) - jnp.log1p(jnp.exp(-jnp.abs(fx)))
    ri = lax.broadcasted_iota(jnp.int32, (L, L), 0)
    ci = lax.broadcasted_iota(jnp.int32, (L, L), 1)
    eye = ri == ci
    tril = ci <= ri
    triu = ri <= ci

    for h in range(MLSTM_HEADS):
        hs = slice(h * MLSTM_DH, (h + 1) * MLSTM_DH)
        qc = _pad_rows(q_ref[:, hs], L)
        kc = _pad_rows(k_ref[:, hs], L) * jnp.asarray(MLSTM_DH ** -0.5, BF16)
        vc = _pad_rows(v_ref[:, hs], L)
        li_r = li_all[h:h + 1, :]
        lf_r = lf_all[h:h + 1, :]
        li_c = jnp.sum(jnp.where(eye, li_r, 0.0), axis=1, keepdims=True)
        lf_c = jnp.sum(jnp.where(eye, lf_r, 0.0), axis=1, keepdims=True)
        bcum_c = jnp.sum(jnp.where(tril, lf_r, 0.0), axis=1, keepdims=True)
        bcum_r = jnp.sum(jnp.where(triu, lf_c, 0.0), axis=0, keepdims=True)
        m_prev = m_ref[h:h + 1, :]
        c_prev = c_ref[h]
        n_prev = n_ref[h:h + 1, :]

        dmat = jnp.where(tril, bcum_c - bcum_r + li_r, NEG)
        m_t = jnp.maximum(bcum_c + m_prev, jnp.max(dmat, axis=1, keepdims=True))
        inter = jnp.exp(bcum_c + m_prev - m_t)
        wts = jnp.exp(dmat - m_t)
        s = _nt_dot(qc, kc) * wts
        num = inter * jnp.dot(qc, c_prev.astype(BF16), preferred_element_type=F32) \
            + jnp.dot(s.astype(BF16), vc, preferred_element_type=F32)
        qn = jnp.sum(qc.astype(F32) * n_prev, axis=1, keepdims=True)
        den = inter * qn + jnp.sum(s, axis=1, keepdims=True)
        hout = num / jnp.maximum(jnp.abs(den), jnp.exp(-m_t))
        hout = hout[:rows]
        ms = jnp.mean(hout * hout, axis=-1, keepdims=True)
        hn = hout * lax.rsqrt(ms + EPS) * g_ref[:, hs]
        gate = jax.nn.sigmoid(og_ref[:, hs].astype(F32))
        h_ref[:, hs] = (hn * gate).astype(h_ref.dtype)

        m_last = m_t[L - 1:L, :]
        b_last = bcum_c[L - 1:L, :]
        w_last = jnp.exp(b_last - bcum_c + li_c - m_last)
        decay = jnp.exp(b_last + m_prev - m_last)
        kw = kc.astype(F32) * w_last
        c_ref[h] = decay * c_prev + _tn_dot(kw.astype(BF16), vc)
        n_ref[h:h + 1, :] = decay * n_prev + jnp.sum(kw, axis=0, keepdims=True)
        m_ref[h:h + 1, :] = m_last


def _mlstm(mqkvo, zt, bias, gain, init, init_layer, prev, layer, *, nb, n_chunks, rows, L, out_rows):
    def tok_spec(col):
        return pl.BlockSpec((rows, MLSTM_W), lambda b, c: (b * n_chunks + c, col))

    def state_specs(li):
        return [pl.BlockSpec((None, None, MLSTM_HEADS, MLSTM_DH, MLSTM_DH), lambda b, c: (li, b, 0, 0, 0)),
                pl.BlockSpec((None, None, MLSTM_HEADS, MLSTM_DH), lambda b, c: (li, b, 0, 0)),
                pl.BlockSpec((None, None, MLSTM_HEADS, 1), lambda b, c: (li, b, 0, 0))]

    prev = tuple(prev) if prev is not None else ()
    n_in = 10
    return pl.pallas_call(
        functools.partial(_mlstm_kernel, L=L, rows=rows),
        grid=(nb, n_chunks),
        in_specs=[tok_spec(0), tok_spec(1), tok_spec(2), tok_spec(3),
                  pl.BlockSpec((None, 2 * MLSTM_HEADS, L), lambda b, c: (b, 0, c)),
                  pl.BlockSpec((2 * MLSTM_HEADS, 1), lambda b, c: (0, 0)),
                  pl.BlockSpec((1, MLSTM_W), lambda b, c: (0, 0))] + state_specs(init_layer)
        + [pl.BlockSpec(memory_space=pl.ANY)] * len(prev),
        out_specs=[pl.BlockSpec((rows, MLSTM_W), lambda b, c: (b * n_chunks + c, 0))] + state_specs(layer),
        out_shape=[jax.ShapeDtypeStruct((out_rows, MLSTM_W), BF16),
                   jax.ShapeDtypeStruct((DEPTH, nb, MLSTM_HEADS, MLSTM_DH, MLSTM_DH), F32),
                   jax.ShapeDtypeStruct((DEPTH, nb, MLSTM_HEADS, MLSTM_DH), F32),
                   jax.ShapeDtypeStruct((DEPTH, nb, MLSTM_HEADS, 1), F32)],
        input_output_aliases={n_in + i: 1 + i for i in range(len(prev))},
        compiler_params=_params(("parallel", "arbitrary")),
        name="mlstm",
    )(mqkvo, mqkvo, mqkvo, mqkvo, zt, bias, gain, *init, *prev)


def _merge_kernel(xa_ref, xb_ref, xc_ref, wa_ref, wb_ref, wc_ref, ga_ref, gb_ref, gc_ref, o_ref):
    out = None
    for x_ref, w_ref, g_ref in ((xa_ref, wa_ref, ga_ref), (xb_ref, wb_ref, gb_ref), (xc_ref, wc_ref, gc_ref)):
        br = jnp.dot(x_ref[...], w_ref[...], preferred_element_type=F32)
        term = jax.nn.sigmoid(g_ref[...].astype(F32)) * br
        out = term if out is None else out + term
    o_ref[...] = out.astype(o_ref.dtype)


def _merge(xs, ws, zg, layer, tm=1088, tn=512):
    nj = D_MODEL // tn
    assert NT % tm == 0 and D_MODEL % tn == 0
    x_spec = pl.BlockSpec((tm, D_MODEL), lambda i, j: (i, 0))
    w_spec = pl.BlockSpec((None, D_MODEL, tn), lambda i, j: (layer, 0, j))
    return pl.pallas_call(
        _merge_kernel,
        grid=(NT // tm, nj),
        in_specs=[x_spec] * N_BRANCH + [w_spec] * N_BRANCH
        + [pl.BlockSpec((tm, tn), functools.partial(lambda i, j, b: (i, b * nj + j), b=b)) for b in range(N_BRANCH)],
        out_specs=pl.BlockSpec((tm, tn), lambda i, j: (i, j)),
        out_shape=jax.ShapeDtypeStruct((NT, D_MODEL), BF16),
        compiler_params=_params(("parallel", "parallel")),
        name="branch_merge",
    )(*xs, *ws, zg, zg, zg)


def _add_residual(acc, r_ref):
    return acc + r_ref[...]


def _relu2(acc):
    return jnp.square(jnp.maximum(acc, 0.0))


def _layer(x, l, rope, mstate, cache_latent, cache_krope_t, page_table, state_C, state_n, state_m, state_pool,
           g_mix, w_in, kv_norm_g, w_ukv, w_pool, pool_scale, b_i, b_f, mlstm_norm_g,
           w_proj_attn, w_proj_pool, w_proj_mlstm, w_out, g_mlp, w_up, w_down):
    c_tab, s1_tab, s2_tab = rope
    tm = 1088
    rope_specs = [(t, pl.BlockSpec((tm, LANE), lambda i, j, k: (i, 0))) for t in (c_tab, s1_tab, s2_tab)]

    wl = w_in[l]
    o = np.cumsum((0, Q_COLS, KV_COLS, POOL_W, 4 * MLSTM_W, 2 * MLSTM_HEADS, N_BRANCH * D_MODEL)).tolist()
    wq = wl[:, o[0]:o[1]].reshape(D_MODEL, MLA_HEADS, QK_NOPE + QK_ROPE)
    wq = jnp.pad(wq, ((0, 0), (0, 0), (0, QH - QK_NOPE - QK_ROPE))).reshape(D_MODEL, MLA_HEADS * QH).astype(BF16)
    wkv = jnp.pad(wl[:, o[1]:o[2]], ((0, 0), (0, KVP - KV_COLS))).astype(BF16)
    wu = wl[:, o[2]:o[3]].astype(BF16)
    wm = wl[:, o[3]:o[4]].astype(BF16)
    wif = jnp.pad(wl[:, o[4]:o[5]], ((0, 0), (0, LANE - 2 * MLSTM_HEADS))).astype(BF16)
    wg = wl[:, o[5]:o[6]].astype(BF16)

    w_uk = w_ukv[l][..., :QK_NOPE]
    w_uv = w_ukv[l][..., QK_NOPE:]
    eye_r = jnp.eye(QK_ROPE, dtype=F32)
    k_top = jnp.pad(w_uk, ((0, 0), (0, 0), (0, QH - QK_NOPE)))
    k_bot = jnp.broadcast_to(jnp.pad(eye_r, ((0, LANE - QK_ROPE), (QK_NOPE, QH - QK_NOPE - QK_ROPE)))[:, None, :],
                             (LANE, MLA_HEADS, QH))
    w_kb = jnp.concatenate([k_top, k_bot], axis=0).reshape(KVP, MLA_HEADS * QH).astype(BF16)
    w_vt = jnp.pad(jnp.transpose(w_uv, (1, 2, 0)).reshape(MLA_HEADS * V_HEAD, KV_RANK),
                   ((0, 0), (0, KVP - KV_RANK))).astype(BF16)
    a_top = jnp.pad(jnp.transpose(w_uk, (1, 2, 0)), ((0, 0), (0, 0), (0, KVP - KV_RANK)))
    a_bot = jnp.broadcast_to(jnp.pad(eye_r, ((0, QH - QK_NOPE - QK_ROPE), (KV_RANK, KVP - KV_RANK - QK_ROPE)))[None],
                             (MLA_HEADS, LANE, KVP))
    w_abs = jnp.concatenate([a_top, a_bot], axis=1).astype(BF16)
    w_uvh = jnp.transpose(w_uv, (1, 0, 2)).astype(BF16)

    h = _rmsnorm(x, g_mix[l], BF16)

    q = _matmul(h, wq, tm=tm, tn=4 * QH, out_dtype=BF16, name="q_proj", epilogue=_q_epilogue, extras=rope_specs)
    ckv = _matmul(h, wkv, tm=tm, tn=KVP, out_dtype=F32, name="kv_proj", epilogue=_kv_epilogue,
                  extras=[(kv_norm_g[l].reshape(1, KV_RANK), pl.BlockSpec((1, KV_RANK), lambda i, j, k: (0, 0)))]
                  + rope_specs)
    tq = 512
    k_p = _matmul(ckv, w_kb, rows=NP, tm=2048, tn=2048, out_dtype=BF16, name="k_up")
    vt_p = _vt_matmul(w_vt, ckv, rows=NP, tt=tq)
    q_abs = _blockdiag_matmul(q[NP:], w_abs, BF16, "q_absorb")
    o_p, o_lat = _attention(q, k_p, vt_p, q_abs.reshape(DEC_BATCH, DEC_SEQ * MLA_HEADS, KVP),
                            ckv[NP:].reshape(DEC_BATCH, DEC_SEQ, KVP), cache_latent, cache_krope_t, page_table, l,
                            batch=BATCH, seq=SEQ, heads=MLA_HEADS, tq=tq, out_rows=NT)
    o_s = _blockdiag_matmul(o_lat.reshape(NS, MLA_HEADS * KV_RANK), w_uvh, BF16, "v_up_sample")
    o_attn = o_p.at[NP:].set(o_s)

    u = _matmul(h, wu, tm=2 * tm, tn=512, out_dtype=F32, name="pool_in")
    maps = w_pool[l].astype(BF16)
    scale = pool_scale[l].reshape(1, POOL_W)
    u_s = u[NP:].reshape(DEC_BATCH, DEC_SEQ, POOL_W)
    ctx_s = jnp.concatenate([state_pool[l], u_s], axis=1)
    pm_s = _pool_sample(jnp.transpose(ctx_s, (1, 0, 2)), maps, scale)
    pm = _pool_prompt(u, maps, scale).at[NP:].set(jnp.transpose(pm_s, (1, 0, 2)).reshape(NS, POOL_W))

    mz = _matmul(h, wm, tm=2 * tm, tn=1024, out_dtype=BF16, name="mlstm_in")
    zif = _matmul(h, wif, tm=tm, tn=LANE, out_dtype=F32, name="gate_in")[:, :2 * MLSTM_HEADS]
    bias = jnp.concatenate([b_i[l], b_f[l]]).reshape(2 * MLSTM_HEADS, 1)
    gain = mlstm_norm_g[l].reshape(1, MLSTM_W)
    lp, ls, rs = 512, LANE, 16
    zt_p = jnp.transpose(zif[:NP].reshape(BATCH, SEQ, 2 * MLSTM_HEADS), (0, 2, 1))
    zero_c = jnp.zeros((1, BATCH, MLSTM_HEADS, MLSTM_DH, MLSTM_DH), F32)
    p_prev, s_prev = mstate if mstate is not None else (None, None)
    hm_p, *p_new = _mlstm(mz, zt_p, bias, gain, (zero_c, zero_c[..., 0], zero_c[..., :1, 0]), 0, p_prev, l,
                          nb=BATCH, n_chunks=SEQ // lp, rows=lp, L=lp, out_rows=NT)
    mz_s = jnp.pad(mz[NP:].reshape(DEC_BATCH, DEC_SEQ, 4 * MLSTM_W), ((0, 0), (0, rs - DEC_SEQ), (0, 0)))
    zt_s = jnp.transpose(zif[NP:].reshape(DEC_BATCH, DEC_SEQ, 2 * MLSTM_HEADS), (0, 2, 1))
    pad_i = jnp.full((DEC_BATCH, MLSTM_HEADS, ls - DEC_SEQ), NEG, F32)
    zt_s = jnp.concatenate([zt_s, jnp.concatenate([pad_i, -pad_i], axis=1)], axis=2)
    hm_s, *s_new = _mlstm(mz_s.reshape(DEC_BATCH * rs, 4 * MLSTM_W), zt_s, bias, gain,
                          (state_C, state_n, state_m[..., None]), l, s_prev, l,
                          nb=DEC_BATCH, n_chunks=1, rows=rs, L=ls, out_rows=DEC_BATCH * rs)
    hm_s = hm_s.reshape(DEC_BATCH, rs, MLSTM_W)[:, :DEC_SEQ].reshape(NS, MLSTM_W)
    hm = hm_p.at[NP:].set(hm_s)

    zg = _matmul(h, wg, tm=2 * tm, tn=1024, out_dtype=BF16, name="branch_gates")
    merged = _merge((o_attn, pm, hm), [w.astype(BF16) for w in (w_proj_attn, w_proj_pool, w_proj_mlstm)], zg, l)
    x = _matmul(merged, w_out.astype(BF16), layer=l, tm=2 * tm, tn=512, out_dtype=F32, name="out_proj",
                epilogue=_add_residual, extras=[(x, pl.BlockSpec((2 * tm, 512), lambda i, j, k: (i, j)))])

    hf = _rmsnorm(x, g_mlp[l], BF16)
    a = _matmul(hf, w_up.astype(BF16), layer=l, tm=2 * tm, tn=1024, out_dtype=BF16, name="mlp_up", epilogue=_relu2)
    x = _matmul(a, w_down.astype(BF16), layer=l, tm=tm // 2, tn=512, out_dtype=F32, name="mlp_down",
                epilogue=_add_residual, extras=[(x, pl.BlockSpec((tm // 2, 512), lambda i, j, k: (i, j)))])

    p_state = (ckv[:NP, :KV_RANK].reshape(BATCH, SEQ, KV_RANK),
               ckv[:NP, KV_RANK:KV_COLS].reshape(BATCH, SEQ, QK_ROPE),
               jnp.stack([u[(b + 1) * SEQ - POOL_CTX:(b + 1) * SEQ] for b in range(BATCH)]))
    s_state = (ckv[NP:, :KV_RANK].reshape(DEC_BATCH, DEC_SEQ, KV_RANK),
               ckv[NP:, KV_RANK:KV_COLS].reshape(DEC_BATCH, DEC_SEQ, QK_ROPE),
               ctx_s[:, -POOL_CTX:])
    return x, p_state, s_state, (p_new, s_new)


def kernel(x_prompt, x_sample, cache_latent, cache_krope, page_table, state_C, state_n, state_m, state_pool,
           g_mix, w_in, kv_norm_g, w_ukv, w_pool, pool_scale, b_i, b_f, mlstm_norm_g,
           w_proj_attn, w_proj_pool, w_proj_mlstm, w_out, g_mlp, w_up, w_down, g_final):
    x = jnp.concatenate([x_prompt.reshape(NP, D_MODEL), x_sample.reshape(NS, D_MODEL)], axis=0).astype(F32)
    rope = _rope_tables()
    cache_krope_t = jnp.swapaxes(cache_krope, 2, 3)
    p_states, s_states, mstate = [], [], None
    for l in range(DEPTH):
        x, ps, ss, mstate = _layer(x, l, rope, mstate, cache_latent, cache_krope_t, page_table, state_C, state_n,
                                   state_m, state_pool, g_mix, w_in, kv_norm_g, w_ukv, w_pool, pool_scale, b_i, b_f,
                                   mlstm_norm_g, w_proj_attn, w_proj_pool, w_proj_mlstm, w_out, g_mlp, w_up, w_down)
        p_states.append(ps)
        s_states.append(ss)
    y_prompt = _rmsnorm(x, g_final, F32, tm=512, rows=NP).reshape(BATCH, SEQ, D_MODEL).astype(x_prompt.dtype)
    y_sample = _rmsnorm(x[NP:], g_final, F32, tm=NS).reshape(DEC_BATCH, DEC_SEQ, D_MODEL).astype(x_sample.dtype)

    def stack(states, i, like):
        return jnp.stack([s[i] for s in states]).astype(like.dtype)

    (p_c, p_n, p_m), (s_c, s_n, s_m) = mstate
    return (y_prompt, y_sample,
            stack(p_states, 0, cache_latent), stack(p_states, 1, cache_krope),
            p_c.astype(state_C.dtype), p_n.astype(state_n.dtype), p_m[..., 0].astype(state_m.dtype),
            stack(p_states, 2, state_pool),
            stack(s_states, 0, cache_latent), stack(s_states, 1, cache_krope),
            s_c.astype(state_C.dtype), s_n.astype(state_n.dtype), s_m[..., 0].astype(state_m.dtype),
            stack(s_states, 2, state_pool))
```

```python
import functools

import numpy as np
import jax
import jax.numpy as jnp
from jax import lax
from jax.experimental import pallas as pl
from jax.experimental.pallas import tpu as pltpu

D_MODEL = 2048
BATCH = 2
SEQ = 4096
DEPTH = 2
DEC_BATCH = 128
DEC_SEQ = 4
PAST_LEN = 8192
PAGE_SIZE = 128
N_PAGES = PAST_LEN // PAGE_SIZE

MLA_HEADS = 16
QK_NOPE = 128
QK_ROPE = 64
V_HEAD = 128
KV_RANK = 512
ROPE_THETA = 10000.0
ATTN_SCALE = (QK_NOPE + QK_ROPE) ** -0.5
LOG2E = 1.4426950408889634
Q_COLS = MLA_HEADS * (QK_NOPE + QK_ROPE)
KV_COLS = KV_RANK + QK_ROPE

POOL_WINDOWS = (2, 4, 8, 16)
POOL_W = 2048
POOL_GROUP = POOL_W // len(POOL_WINDOWS)
POOL_CTX = max(POOL_WINDOWS) - 1

MLSTM_HEADS = 8
MLSTM_DH = 256
MLSTM_W = MLSTM_HEADS * MLSTM_DH
N_BRANCH = 3
D_FF = 4 * D_MODEL
EPS = 1e-6

NP = BATCH * SEQ
NS = DEC_BATCH * DEC_SEQ
NT = NP + NS

LANE = 128
QH = 2 * LANE
KVP = KV_RANK + LANE
NEG = -1e30
ONES_ROWS = 16
VMEM_LIMIT = 56 * 1024 * 1024

F32 = jnp.float32
BF16 = jnp.bfloat16


def _params(sem, vmem=VMEM_LIMIT):
    return pltpu.CompilerParams(dimension_semantics=sem, vmem_limit_bytes=vmem)


def _nt_dot(a, b):
    return lax.dot_general(a, b, (((1,), (1,)), ((), ())), preferred_element_type=F32)


def _tn_dot(a, b):
    return lax.dot_general(a, b, (((0,), (0,)), ((), ())), preferred_element_type=F32)


def _rmsnorm_kernel(x_ref, g_ref, o_ref):
    x = x_ref[...]
    ms = jnp.mean(x * x, axis=-1, keepdims=True)
    o_ref[...] = (x * lax.rsqrt(ms + EPS) * g_ref[...]).astype(o_ref.dtype)


def _rmsnorm(x, g, out_dtype, tm=544, rows=None):
    m = rows or x.shape[0]
    d = x.shape[1]
    assert m % tm == 0, (m, tm)
    return pl.pallas_call(
        _rmsnorm_kernel,
        grid=(m // tm,),
        in_specs=[pl.BlockSpec((tm, d), lambda i: (i, 0)), pl.BlockSpec((1, d), lambda i: (0, 0))],
        out_specs=pl.BlockSpec((tm, d), lambda i: (i, 0)),
        out_shape=jax.ShapeDtypeStruct((m, d), out_dtype),
        compiler_params=_params(("parallel",)),
        name="rmsnorm",
    )(x, g.reshape(1, d))


def _matmul_kernel(*refs, n_extra, nk, epilogue):
    x_ref, w_ref = refs[0], refs[1]
    extra = refs[2:2 + n_extra]
    o_ref = refs[2 + n_extra]
    part = jnp.dot(x_ref[...].astype(BF16), w_ref[...], preferred_element_type=F32)
    if nk == 1:
        o_ref[...] = epilogue(part, *extra).astype(o_ref.dtype)
        return
    acc_ref = refs[3 + n_extra]
    k = pl.program_id(2)

    @pl.when(k == 0)
    def _():
        acc_ref[...] = part

    @pl.when(k > 0)
    def _():
        acc_ref[...] += part

    @pl.when(k == nk - 1)
    def _():
        o_ref[...] = epilogue(acc_ref[...], *extra).astype(o_ref.dtype)


def _matmul(x, w, *, tm, tn, out_dtype, name, tk=None, rows=None, layer=None, epilogue=None, extras=()):
    m = rows or x.shape[0]
    kdim, n = w.shape[-2:]
    tk = tk or kdim
    nk = kdim // tk
    assert m % tm == 0 and n % tn == 0 and kdim % tk == 0, (m, n, kdim, tm, tn, tk)
    if epilogue is None:
        epilogue = lambda acc: acc
    if layer is None:
        w_spec = pl.BlockSpec((tk, tn), lambda i, j, k: (k, j))
    else:
        w_spec = pl.BlockSpec((None, tk, tn), lambda i, j, k: (layer, k, j))
    kern = functools.partial(_matmul_kernel, n_extra=len(extras), nk=nk, epilogue=epilogue)
    return pl.pallas_call(
        kern,
        grid=(m // tm, n // tn, nk),
        in_specs=[pl.BlockSpec((tm, tk), lambda i, j, k: (i, k)), w_spec] + [s for _, s in extras],
        out_specs=pl.BlockSpec((tm, tn), lambda i, j, k: (i, j)),
        out_shape=jax.ShapeDtypeStruct((m, n), out_dtype),
        scratch_shapes=[pltpu.VMEM((tm, tn), F32)] if nk > 1 else [],
        compiler_params=_params(("parallel", "parallel", "arbitrary")),
        name=name,
    )(x, w, *[a for a, _ in extras])


def _blockdiag_kernel(x_ref, w_ref, o_ref):
    o_ref[...] = jnp.dot(x_ref[...], w_ref[...], preferred_element_type=F32).astype(o_ref.dtype)


def _blockdiag_matmul(x, w, out_dtype, name):
    m = x.shape[0]
    g, kg, ng = w.shape
    return pl.pallas_call(
        _blockdiag_kernel,
        grid=(g,),
        in_specs=[pl.BlockSpec((m, kg), lambda i: (0, i)), pl.BlockSpec((None, kg, ng), lambda i: (i, 0, 0))],
        out_specs=pl.BlockSpec((m, ng), lambda i: (0, i)),
        out_shape=jax.ShapeDtypeStruct((m, g * ng), out_dtype),
        compiler_params=_params(("parallel",)),
        name=name,
    )(x, w)


def _rope_lanes(pe, c_ref, s1_ref, s2_ref):
    return (pe * c_ref[...] + pltpu.roll(pe, LANE - QK_ROPE // 2, 1) * s1_ref[...]
            + pltpu.roll(pe, QK_ROPE // 2, 1) * s2_ref[...])


def _rope_tables():
    pos = jnp.concatenate([jnp.tile(jnp.arange(SEQ, dtype=F32), BATCH),
                           jnp.tile(PAST_LEN + jnp.arange(DEC_SEQ, dtype=F32), DEC_BATCH)])
    inv = ROPE_THETA ** (-jnp.arange(0, QK_ROPE, 2, dtype=F32) / QK_ROPE)
    ang = pos[:, None] * inv[None, :]
    cos, sin = jnp.cos(ang), jnp.sin(ang)
    z = jnp.zeros_like(cos)
    c = jnp.concatenate([cos, cos, z, z], axis=1)
    s1 = jnp.concatenate([-sin, z, z, z], axis=1)
    s2 = jnp.concatenate([z, sin, z, z], axis=1)
    return c, s1, s2


def _q_epilogue(acc, c_ref, s1_ref, s2_ref):
    parts = []
    for h in range(acc.shape[1] // QH):
        parts.append(acc[:, h * QH:h * QH + LANE])
        parts.append(_rope_lanes(acc[:, h * QH + LANE:(h + 1) * QH], c_ref, s1_ref, s2_ref))
    return jnp.concatenate(parts, axis=1) * (ATTN_SCALE * LOG2E)


def _kv_epilogue(acc, g_ref, c_ref, s1_ref, s2_ref):
    lat = acc[:, :KV_RANK]
    ms = jnp.mean(lat * lat, axis=-1, keepdims=True)
    lat = lat * lax.rsqrt(ms + EPS) * g_ref[...]
    return jnp.concatenate([lat, _rope_lanes(acc[:, KV_RANK:], c_ref, s1_ref, s2_ref)], axis=1)


def _vt_kernel(w_ref, x_ref, o_ref):
    o_ref[...] = _nt_dot(w_ref[...], x_ref[...].astype(BF16)).astype(o_ref.dtype)


def _vt_matmul(w_t, x, *, rows, tt, tn=2048):
    n, kdim = w_t.shape
    return pl.pallas_call(
        _vt_kernel,
        grid=(rows // tt, n // tn),
        in_specs=[pl.BlockSpec((tn, kdim), lambda j, i: (i, 0)), pl.BlockSpec((tt, kdim), lambda j, i: (j, 0))],
        out_specs=pl.BlockSpec((None, tn, tt), lambda j, i: (j, i, 0)),
        out_shape=jax.ShapeDtypeStruct((rows // tt, n, tt), BF16),
        compiler_params=_params(("parallel", "parallel")),
        name="v_up_t",
    )(w_t, x)


def _attn_block(iq, q_ref, k_ref, vt_ref, o_ref, s0_ref, s1_ref, m_ref, acc_ref, tq):
    q = q_ref[...]
    m_ref[...] = jnp.full_like(m_ref, NEG)
    acc_ref[...] = jnp.zeros_like(acc_ref)

    def scores(ik, s_ref):
        start = pl.multiple_of(ik * tq, tq)
        s_ref[...] = _nt_dot(k_ref[pl.ds(start, tq), :], q)

    def update(ik, s_ref, diagonal):
        s = s_ref[...]
        if diagonal:
            ki = lax.broadcasted_iota(jnp.int32, s.shape, 0)
            qi = lax.broadcasted_iota(jnp.int32, s.shape, 1)
            s = jnp.where(ki <= qi, s, NEG)
        m_prev = m_ref[...]
        m_new = jnp.maximum(m_prev, jnp.max(s, axis=0, keepdims=True))
        alpha = jnp.exp2(m_prev - m_new)
        p = jnp.exp2(s - m_new).astype(BF16)
        vt1 = jnp.concatenate([vt_ref[ik], jnp.ones((ONES_ROWS, tq), BF16)], axis=0)
        acc_ref[...] = alpha * acc_ref[...] + jnp.dot(vt1, p, preferred_element_type=F32)
        m_ref[...] = m_new

    scores(0, s0_ref)

    def body(j, carry):
        ik = 2 * j
        scores(ik + 1, s1_ref)
        update(ik, s0_ref, False)
        scores(ik + 2, s0_ref)
        update(ik + 1, s1_ref, False)
        return carry

    lax.fori_loop(0, lax.div(iq, 2), body, 0)

    @pl.when(lax.rem(iq, 2) == 0)
    def _():
        update(iq, s0_ref, True)

    @pl.when(lax.rem(iq, 2) == 1)
    def _():
        scores(iq, s1_ref)
        update(iq - 1, s0_ref, False)
        update(iq, s1_ref, True)

    out = acc_ref[:V_HEAD, :] / acc_ref[V_HEAD:V_HEAD + 1, :]
    o_ref[...] = jnp.transpose(out).astype(o_ref.dtype)


N_SLOTS = 3


def _page_copies(pt_ref, lat_hbm, kr_hbm, lat_buf, kr_buf, sem, d, slot, pg, layer):
    cps = []
    for i in range(pg):
        page = pt_ref[d * pg + i]
        cps.append(pltpu.make_async_copy(lat_hbm.at[layer, page], lat_buf.at[slot, i], sem.at[0, slot]))
        cps.append(pltpu.make_async_copy(kr_hbm.at[layer, page], kr_buf.at[slot, i], sem.at[1, slot]))
    return cps


def _decode_group(q_ref, lat_buf, kr_buf, slot, m_ref, l_ref, acc_ref, pg, n_chains):
    q = q_ref[...]
    q_lat, q_pe = q[:, :KV_RANK], q[:, KV_RANK:]
    per = pg // n_chains
    lats, scores = [], []
    for c in range(n_chains):
        pages = range(c * per, (c + 1) * per)
        lat = jnp.concatenate([lat_buf[slot, i].astype(BF16) for i in pages], axis=0)
        krt = jnp.concatenate([kr_buf[slot, i] for i in pages], axis=1)
        krt = jnp.concatenate([krt, jnp.zeros_like(krt)], axis=0).astype(BF16)
        lats.append(lat)
        scores.append(_nt_dot(q_lat, lat) + jnp.dot(q_pe, krt, preferred_element_type=F32))
    for c in range(n_chains):
        s, m_prev = scores[c], m_ref[c]
        m_new = jnp.maximum(m_prev, jnp.max(s, axis=-1, keepdims=True))
        alpha = jnp.exp2(m_prev - m_new)
        p = jnp.exp2(s - m_new)
        l_ref[c] = alpha * l_ref[c] + jnp.sum(p, axis=-1, keepdims=True)
        acc_ref[c] = alpha * acc_ref[c] + jnp.dot(p.astype(BF16), lats[c], preferred_element_type=F32)
        m_ref[c] = m_new


def _decode_finish(q_ref, new_ref, o_ref, m_ref, l_ref, acc_ref, n_chains):
    m_run = m_ref[0]
    for c in range(1, n_chains):
        m_run = jnp.maximum(m_run, m_ref[c])
    l_run = jnp.zeros_like(m_run)
    acc = jnp.zeros(acc_ref.shape[1:], F32)
    for c in range(n_chains):
        w = jnp.exp2(m_ref[c] - m_run)
        l_run = l_run + w * l_ref[c]
        acc = acc + w * acc_ref[c]
    qf = q_ref[...].astype(F32)
    row = lax.broadcasted_iota(jnp.int32, m_run.shape, 0)
    for t in range(DEC_SEQ):
        kn = new_ref[t:t + 1, :]
        st = jnp.sum(qf * kn, axis=-1, keepdims=True)
        st = jnp.where(row >= t * MLA_HEADS, st, NEG)
        m_nxt = jnp.maximum(m_run, st)
        a = jnp.exp2(m_run - m_nxt)
        pt = jnp.exp2(st - m_nxt)
        l_run = a * l_run + pt
        acc = a * acc + pt * kn[:, :KV_RANK]
        m_run = m_nxt
    o_ref[...] = (acc / l_run).astype(o_ref.dtype)


def _query_block(i, nq):
    return jnp.where(lax.rem(i, 2) == 0, nq - 1 - lax.div(i, 2), lax.div(i, 2))


def _attention_kernel(pt_ref, q_ref, k_ref, vt_ref, qd_ref, new_ref, lat_hbm, kr_hbm, o_ref, od_ref,
                      s0_ref, s1_ref, m_ref, acc_ref, lat_buf, kr_buf, sem, md_ref, ld_ref, accd_ref,
                      *, tq, pg, n_chains, layer, n_steps):
    f = (pl.program_id(0) * pl.num_programs(1) + pl.program_id(1)) * pl.num_programs(2) + pl.program_id(2)
    d_a, d_b = 2 * f, 2 * f + 1

    def copies(d):
        return _page_copies(pt_ref, lat_hbm, kr_hbm, lat_buf, kr_buf, sem, d, lax.rem(d, N_SLOTS), pg, layer)

    @pl.when(f == 0)
    def _():
        for cp in copies(d_a):
            cp.start()

    for cp in copies(d_b):
        cp.start()

    _attn_block(_query_block(pl.program_id(2), pl.num_programs(2)), q_ref, k_ref, vt_ref, o_ref,
                s0_ref, s1_ref, m_ref, acc_ref, tq)

    @pl.when(f + 1 < n_steps)
    def _():
        for cp in copies(d_a + 2):
            cp.start()

    @pl.when(lax.rem(f, 2) == 0)
    def _():
        md_ref[...] = jnp.full_like(md_ref, NEG)
        ld_ref[...] = jnp.zeros_like(ld_ref)
        accd_ref[...] = jnp.zeros_like(accd_ref)

    for d in (d_a, d_b):
        for cp in copies(d):
            cp.wait()
        _decode_group(qd_ref, lat_buf, kr_buf, lax.rem(d, N_SLOTS), md_ref, ld_ref, accd_ref, pg, n_chains)

    @pl.when(lax.rem(f, 2) == 1)
    def _():
        _decode_finish(qd_ref, new_ref, od_ref, md_ref, ld_ref, accd_ref, n_chains)


def _attention(q, k, vt, q_abs, ckv_new, cache_latent, cache_krope_t, page_table, layer, *,
               batch, seq, heads, tq, out_rows, pg=16, n_chains=4):
    nq = seq // tq
    n_steps = batch * heads * nq
    rows = DEC_SEQ * MLA_HEADS
    assert N_PAGES == 4 * pg and 2 * n_steps == DEC_BATCH * (N_PAGES // pg), "two page groups per prompt grid step"

    def seq_of(b, h, i):
        return ((b * heads + h) * nq + i) // 2

    grid_spec = pltpu.PrefetchScalarGridSpec(
        num_scalar_prefetch=1,
        grid=(batch, heads, nq),
        in_specs=[pl.BlockSpec((tq, QH), lambda b, h, i, pt: (b * nq + _query_block(i, nq), h)),
                  pl.BlockSpec((seq, QH), lambda b, h, i, pt: (b, h)),
                  pl.BlockSpec((nq, V_HEAD, tq), lambda b, h, i, pt: (b, h, 0)),
                  pl.BlockSpec((None, rows, KVP), lambda b, h, i, pt: (seq_of(b, h, i), 0, 0)),
                  pl.BlockSpec((None, DEC_SEQ, KVP), lambda b, h, i, pt: (seq_of(b, h, i), 0, 0)),
                  pl.BlockSpec(memory_space=pl.ANY),
                  pl.BlockSpec(memory_space=pl.ANY)],
        out_specs=[pl.BlockSpec((tq, V_HEAD), lambda b, h, i, pt: (b * nq + _query_block(i, nq), h)),
                   pl.BlockSpec((None, rows, KV_RANK), lambda b, h, i, pt: (seq_of(b, h, i), 0, 0))],
        scratch_shapes=[pltpu.VMEM((tq, tq), F32), pltpu.VMEM((tq, tq), F32),
                        pltpu.VMEM((1, tq), F32), pltpu.VMEM((V_HEAD + ONES_ROWS, tq), F32),
                        pltpu.VMEM((N_SLOTS, pg, PAGE_SIZE, KV_RANK), F32),
                        pltpu.VMEM((N_SLOTS, pg, QK_ROPE, PAGE_SIZE), F32),
                        pltpu.SemaphoreType.DMA((2, N_SLOTS)),
                        pltpu.VMEM((n_chains, rows, 1), F32), pltpu.VMEM((n_chains, rows, 1), F32),
                        pltpu.VMEM((n_chains, rows, KV_RANK), F32)],
    )
    return pl.pallas_call(
        functools.partial(_attention_kernel, tq=tq, pg=pg, n_chains=n_chains, layer=layer, n_steps=n_steps),
        grid_spec=grid_spec,
        out_shape=[jax.ShapeDtypeStruct((out_rows, heads * V_HEAD), BF16),
                   jax.ShapeDtypeStruct((DEC_BATCH, rows, KV_RANK), BF16)],
        compiler_params=_params(("arbitrary", "arbitrary", "arbitrary")),
        name="attention",
    )(page_table.reshape(-1), q, k, vt, q_abs, ckv_new, cache_latent, cache_krope_t)


def _pool_prompt_kernel(u_ref, halo_ref, maps_ref, scale_ref, o_ref, *, tm):
    row0 = lax.rem(pl.program_id(0) * tm, SEQ)
    pos1 = row0 + 1 + lax.broadcasted_iota(jnp.int32, (tm, 1), 0)
    for g, w in enumerate(POOL_WINDOWS):
        sl = slice(g * POOL_GROUP, (g + 1) * POOL_GROUP)
        u = u_ref[:, sl]
        halo = jnp.where(row0 != 0, halo_ref[:, sl], 0.0)
        acc = jnp.concatenate([halo, u], axis=0)
        shift = 1
        while shift < w:
            acc = acc + pltpu.roll(acc, shift, 0)
            shift *= 2
        cnt = jnp.minimum(w, pos1).astype(F32)
        d = acc[POOL_CTX + 1:, :] / cnt - u
        y = jnp.dot(d.astype(BF16), maps_ref[g], preferred_element_type=F32) * scale_ref[:, sl]
        o_ref[:, sl] = y.astype(o_ref.dtype)


def _pool_prompt(u, maps, scale, tm=512):
    hb = POOL_CTX + 1
    return pl.pallas_call(
        functools.partial(_pool_prompt_kernel, tm=tm),
        grid=(NP // tm,),
        in_specs=[pl.BlockSpec((tm, POOL_W), lambda i: (i, 0)),
                  pl.BlockSpec((hb, POOL_W), lambda i: (jnp.maximum(i * (tm // hb) - 1, 0), 0)),
                  pl.BlockSpec(maps.shape, lambda i: (0, 0, 0)),
                  pl.BlockSpec((1, POOL_W), lambda i: (0, 0))],
        out_specs=pl.BlockSpec((tm, POOL_W), lambda i: (i, 0)),
        out_shape=jax.ShapeDtypeStruct((NT, POOL_W), BF16),
        compiler_params=_params(("parallel",)),
        name="pool_prompt",
    )(u, u, maps, scale)


def _pool_sample_kernel(ctx_ref, maps_ref, scale_ref, o_ref):
    for g, w in enumerate(POOL_WINDOWS):
        sl = slice(g * POOL_GROUP, (g + 1) * POOL_GROUP)
        rows = []
        for t in range(DEC_SEQ):
            r = POOL_CTX + t
            acc = ctx_ref[r, :, sl]
            for j in range(1, w):
                acc = acc + ctx_ref[r - j, :, sl]
            cnt = float(min(w, PAST_LEN - POOL_CTX + r + 1))
            rows.append(acc / cnt - ctx_ref[r, :, sl])
        d = jnp.concatenate(rows, axis=0).astype(BF16)
        y = jnp.dot(d, maps_ref[g], preferred_element_type=F32) * scale_ref[:, sl]
        bb = y.shape[0] // DEC_SEQ
        for t in range(DEC_SEQ):
            o_ref[t, :, sl] = y[t * bb:(t + 1) * bb].astype(o_ref.dtype)


def _pool_sample(ctx_t, maps, scale, bb=32):
    tc = POOL_CTX + DEC_SEQ
    return pl.pallas_call(
        _pool_sample_kernel,
        grid=(DEC_BATCH // bb,),
        in_specs=[pl.BlockSpec((tc, bb, POOL_W), lambda i: (0, i, 0)),
                  pl.BlockSpec(maps.shape, lambda i: (0, 0, 0)),
                  pl.BlockSpec((1, POOL_W), lambda i: (0, 0))],
        out_specs=pl.BlockSpec((DEC_SEQ, bb, POOL_W), lambda i: (0, i, 0)),
        out_shape=jax.ShapeDtypeStruct((DEC_SEQ, DEC_BATCH, POOL_W), BF16),
        compiler_params=_params(("parallel",)),
        name="pool_sample",
    )(ctx_t, maps, scale)


def _pad_rows(x, n):
    if x.shape[0] == n:
        return x
    return jnp.concatenate([x, jnp.zeros((n - x.shape[0], x.shape[1]), x.dtype)], axis=0)


def _mlstm_kernel(q_ref, k_ref, v_ref, og_ref, zt_ref, bias_ref, g_ref, c0_ref, n0_ref, m0_ref, *refs, L, rows):
    h_ref, c_ref, n_ref, m_ref = refs[-4:]

    @pl.when(pl.program_id(1) == 0)
    def _():
        c_ref[...] = c0_ref[...]
        n_ref[...] = n0_ref[...]
        m_ref[...] = m0_ref[...]

    zb = zt_ref[...] + bias_ref[...]
    li_all = zb[:MLSTM_HEADS]
    fx = zb[MLSTM_HEADS:]
    lf_all = jnp.minimum(fx, 0.0) - jnp.log1p(jnp.exp(-jnp.abs(fx)))
    ri = lax.broadcasted_iota(jnp.int32, (L, L), 0)
    ci = lax.broadcasted_iota(jnp.int32, (L, L), 1)
    eye = ri == ci
    tril = ci <= ri
    triu = ri <= ci

    for h in range(MLSTM_HEADS):
        hs = slice(h * MLSTM_DH, (h + 1) * MLSTM_DH)
        qc = _pad_rows(q_ref[:, hs], L)
        kc = _pad_rows(k_ref[:, hs], L) * jnp.asarray(MLSTM_DH ** -0.5, BF16)
        vc = _pad_rows(v_ref[:, hs], L)
        li_r = li_all[h:h + 1, :]
        lf_r = lf_all[h:h + 1, :]
        li_c = jnp.sum(jnp.where(eye, li_r, 0.0), axis=1, keepdims=True)
        lf_c = jnp.sum(jnp.where(eye, lf_r, 0.0), axis=1, keepdims=True)
        bcum_c = jnp.sum(jnp.where(tril, lf_r, 0.0), axis=1, keepdims=True)
        bcum_r = jnp.sum(jnp.where(triu, lf_c, 0.0), axis=0, keepdims=True)
        m_prev = m_ref[h:h + 1, :]
        c_prev = c_ref[h]
        n_prev = n_ref[h:h + 1, :]

        dmat = jnp.where(tril, bcum_c - bcum_r + li_r, NEG)
        m_t = jnp.maximum(bcum_c + m_prev, jnp.max(dmat, axis=1, keepdims=True))
        inter = jnp.exp(bcum_c + m_prev - m_t)
        wts = jnp.exp(dmat - m_t)
        s = _nt_dot(qc, kc) * wts
        num = inter * jnp.dot(qc, c_prev.astype(BF16), preferred_element_type=F32) \
            + jnp.dot(s.astype(BF16), vc, preferred_element_type=F32)
        qn = jnp.sum(qc.astype(F32) * n_prev, axis=1, keepdims=True)
        den = inter * qn + jnp.sum(s, axis=1, keepdims=True)
        hout = num / jnp.maximum(jnp.abs(den), jnp.exp(-m_t))
        hout = hout[:rows]
        ms = jnp.mean(hout * hout, axis=-1, keepdims=True)
        hn = hout * lax.rsqrt(ms + EPS) * g_ref[:, hs]
        gate = jax.nn.sigmoid(og_ref[:, hs].astype(F32))
        h_ref[:, hs] = (hn * gate).astype(h_ref.dtype)

        m_last = m_t[L - 1:L, :]
        b_last = bcum_c[L - 1:L, :]
        w_last = jnp.exp(b_last - bcum_c + li_c - m_last)
        decay = jnp.exp(b_last + m_prev - m_last)
        kw = kc.astype(F32) * w_last
        c_ref[h] = decay * c_prev + _tn_dot(kw.astype(BF16), vc)
        n_ref[h:h + 1, :] = decay * n_prev + jnp.sum(kw, axis=0, keepdims=True)
        m_ref[h:h + 1, :] = m_last


def _mlstm(mqkvo, zt, bias, gain, init, init_layer, prev, layer, *, nb, n_chunks, rows, L, out_rows):
    def tok_spec(col):
        return pl.BlockSpec((rows, MLSTM_W), lambda b, c: (b * n_chunks + c, col))

    def state_specs(li):
        return [pl.BlockSpec((None, None, MLSTM_HEADS, MLSTM_DH, MLSTM_DH), lambda b, c: (li, b, 0, 0, 0)),
                pl.BlockSpec((None, None, MLSTM_HEADS, MLSTM_DH), lambda b, c: (li, b, 0, 0)),
                pl.BlockSpec((None, None, MLSTM_HEADS, 1), lambda b, c: (li, b, 0, 0))]

    prev = tuple(prev) if prev is not None else ()
    n_in = 10
    return pl.pallas_call(
        functools.partial(_mlstm_kernel, L=L, rows=rows),
        grid=(nb, n_chunks),
        in_specs=[tok_spec(0), tok_spec(1), tok_spec(2), tok_spec(3),
                  pl.BlockSpec((None, 2 * MLSTM_HEADS, L), lambda b, c: (b, 0, c)),
                  pl.BlockSpec((2 * MLSTM_HEADS, 1), lambda b, c: (0, 0)),
                  pl.BlockSpec((1, MLSTM_W), lambda b, c: (0, 0))] + state_specs(init_layer)
        + [pl.BlockSpec(memory_space=pl.ANY)] * len(prev),
        out_specs=[pl.BlockSpec((rows, MLSTM_W), lambda b, c: (b * n_chunks + c, 0))] + state_specs(layer),
        out_shape=[jax.ShapeDtypeStruct((out_rows, MLSTM_W), BF16),
                   jax.ShapeDtypeStruct((DEPTH, nb, MLSTM_HEADS, MLSTM_DH, MLSTM_DH), F32),
                   jax.ShapeDtypeStruct((DEPTH, nb, MLSTM_HEADS, MLSTM_DH), F32),
                   jax.ShapeDtypeStruct((DEPTH, nb, MLSTM_HEADS, 1), F32)],
        input_output_aliases={n_in + i: 1 + i for i in range(len(prev))},
        compiler_params=_params(("parallel", "arbitrary")),
        name="mlstm",
    )(mqkvo, mqkvo, mqkvo, mqkvo, zt, bias, gain, *init, *prev)


def _merge_kernel(xa_ref, xb_ref, xc_ref, wa_ref, wb_ref, wc_ref, ga_ref, gb_ref, gc_ref, o_ref):
    out = None
    for x_ref, w_ref, g_ref in ((xa_ref, wa_ref, ga_ref), (xb_ref, wb_ref, gb_ref), (xc_ref, wc_ref, gc_ref)):
        br = jnp.dot(x_ref[...], w_ref[...], preferred_element_type=F32)
        term = jax.nn.sigmoid(g_ref[...].astype(F32)) * br
        out = term if out is None else out + term
    o_ref[...] = out.astype(o_ref.dtype)


def _merge(xs, ws, zg, layer, tm=1088, tn=512):
    nj = D_MODEL // tn
    assert NT % tm == 0 and D_MODEL % tn == 0
    x_spec = pl.BlockSpec((tm, D_MODEL), lambda i, j: (i, 0))
    w_spec = pl.BlockSpec((None, D_MODEL, tn), lambda i, j: (layer, 0, j))
    return pl.pallas_call(
        _merge_kernel,
        grid=(NT // tm, nj),
        in_specs=[x_spec] * N_BRANCH + [w_spec] * N_BRANCH
        + [pl.BlockSpec((tm, tn), functools.partial(lambda i, j, b: (i, b * nj + j), b=b)) for b in range(N_BRANCH)],
        out_specs=pl.BlockSpec((tm, tn), lambda i, j: (i, j)),
        out_shape=jax.ShapeDtypeStruct((NT, D_MODEL), BF16),
        compiler_params=_params(("parallel", "parallel")),
        name="branch_merge",
    )(*xs, *ws, zg, zg, zg)


def _add_residual(acc, r_ref):
    return acc + r_ref[...]


def _relu2(acc):
    return jnp.square(jnp.maximum(acc, 0.0))


def _layer(x, l, rope, mstate, cache_latent, cache_krope_t, page_table, state_C, state_n, state_m, state_pool,
           g_mix, w_in, kv_norm_g, w_ukv, w_pool, pool_scale, b_i, b_f, mlstm_norm_g,
           w_proj_attn, w_proj_pool, w_proj_mlstm, w_out, g_mlp, w_up, w_down):
    c_tab, s1_tab, s2_tab = rope
    tm = 1088
    rope_specs = [(t, pl.BlockSpec((tm, LANE), lambda i, j, k: (i, 0))) for t in (c_tab, s1_tab, s2_tab)]

    wl = w_in[l]
    o = np.cumsum((0, Q_COLS, KV_COLS, POOL_W, 4 * MLSTM_W, 2 * MLSTM_HEADS, N_BRANCH * D_MODEL)).tolist()
    wq = wl[:, o[0]:o[1]].reshape(D_MODEL, MLA_HEADS, QK_NOPE + QK_ROPE)
    wq = jnp.pad(wq, ((0, 0), (0, 0), (0, QH - QK_NOPE - QK_ROPE))).reshape(D_MODEL, MLA_HEADS * QH).astype(BF16)
    wkv = jnp.pad(wl[:, o[1]:o[2]], ((0, 0), (0, KVP - KV_COLS))).astype(BF16)
    wu = wl[:, o[2]:o[3]].astype(BF16)
    wm = wl[:, o[3]:o[4]].astype(BF16)
    wif = jnp.pad(wl[:, o[4]:o[5]], ((0, 0), (0, LANE - 2 * MLSTM_HEADS))).astype(BF16)
    wg = wl[:, o[5]:o[6]].astype(BF16)

    w_uk = w_ukv[l][..., :QK_NOPE]
    w_uv = w_ukv[l][..., QK_NOPE:]
    eye_r = jnp.eye(QK_ROPE, dtype=F32)
    k_top = jnp.pad(w_uk, ((0, 0), (0, 0), (0, QH - QK_NOPE)))
    k_bot = jnp.broadcast_to(jnp.pad(eye_r, ((0, LANE - QK_ROPE), (QK_NOPE, QH - QK_NOPE - QK_ROPE)))[:, None, :],
                             (LANE, MLA_HEADS, QH))
    w_kb = jnp.concatenate([k_top, k_bot], axis=0).reshape(KVP, MLA_HEADS * QH).astype(BF16)
    w_vt = jnp.pad(jnp.transpose(w_uv, (1, 2, 0)).reshape(MLA_HEADS * V_HEAD, KV_RANK),
                   ((0, 0), (0, KVP - KV_RANK))).astype(BF16)
    a_top = jnp.pad(jnp.transpose(w_uk, (1, 2, 0)), ((0, 0), (0, 0), (0, KVP - KV_RANK)))
    a_bot = jnp.broadcast_to(jnp.pad(eye_r, ((0, QH - QK_NOPE - QK_ROPE), (KV_RANK, KVP - KV_RANK - QK_ROPE)))[None],
                             (MLA_HEADS, LANE, KVP))
    w_abs = jnp.concatenate([a_top, a_bot], axis=1).astype(BF16)
    w_uvh = jnp.transpose(w_uv, (1, 0, 2)).astype(BF16)

    h = _rmsnorm(x, g_mix[l], BF16)

    q = _matmul(h, wq, tm=tm, tn=4 * QH, out_dtype=BF16, name="q_proj", epilogue=_q_epilogue, extras=rope_specs)
    ckv = _matmul(h, wkv, tm=tm, tn=KVP, out_dtype=F32, name="kv_proj", epilogue=_kv_epilogue,
                  extras=[(kv_norm_g[l].reshape(1, KV_RANK), pl.BlockSpec((1, KV_RANK), lambda i, j, k: (0, 0)))]
                  + rope_specs)
    tq = 512
    k_p = _matmul(ckv, w_kb, rows=NP, tm=2048, tn=2048, out_dtype=BF16, name="k_up")
    vt_p = _vt_matmul(w_vt, ckv, rows=NP, tt=tq)
    q_abs = _blockdiag_matmul(q[NP:], w_abs, BF16, "q_absorb")
    o_p, o_lat = _attention(q, k_p, vt_p, q_abs.reshape(DEC_BATCH, DEC_SEQ * MLA_HEADS, KVP),
                            ckv[NP:].reshape(DEC_BATCH, DEC_SEQ, KVP), cache_latent, cache_krope_t, page_table, l,
                            batch=BATCH, seq=SEQ, heads=MLA_HEADS, tq=tq, out_rows=NT)
    o_s = _blockdiag_matmul(o_lat.reshape(NS, MLA_HEADS * KV_RANK), w_uvh, BF16, "v_up_sample")
    o_attn = o_p.at[NP:].set(o_s)

    u = _matmul(h, wu, tm=2 * tm, tn=512, out_dtype=F32, name="pool_in")
    maps = w_pool[l].astype(BF16)
    scale = pool_scale[l].reshape(1, POOL_W)
    u_s = u[NP:].reshape(DEC_BATCH, DEC_SEQ, POOL_W)
    ctx_s = jnp.concatenate([state_pool[l], u_s], axis=1)
    pm_s = _pool_sample(jnp.transpose(ctx_s, (1, 0, 2)), maps, scale)
    pm = _pool_prompt(u, maps, scale).at[NP:].set(jnp.transpose(pm_s, (1, 0, 2)).reshape(NS, POOL_W))

    mz = _matmul(h, wm, tm=2 * tm, tn=1024, out_dtype=BF16, name="mlstm_in")
    zif = _matmul(h, wif, tm=tm, tn=LANE, out_dtype=F32, name="gate_in")[:, :2 * MLSTM_HEADS]
    bias = jnp.concatenate([b_i[l], b_f[l]]).reshape(2 * MLSTM_HEADS, 1)
    gain = mlstm_norm_g[l].reshape(1, MLSTM_W)
    lp, ls, rs = 512, 16, 16
    zt_p = jnp.transpose(zif[:NP].reshape(BATCH, SEQ, 2 * MLSTM_HEADS), (0, 2, 1))
    zero_c = jnp.zeros((1, BATCH, MLSTM_HEADS, MLSTM_DH, MLSTM_DH), F32)
    p_prev, s_prev = mstate if mstate is not None else (None, None)
    hm_p, *p_new = _mlstm(mz, zt_p, bias, gain, (zero_c, zero_c[..., 0], zero_c[..., :1, 0]), 0, p_prev, l,
                          nb=BATCH, n_chunks=SEQ // lp, rows=lp, L=lp, out_rows=NT)
    mz_s = jnp.pad(mz[NP:].reshape(DEC_BATCH, DEC_SEQ, 4 * MLSTM_W), ((0, 0), (0, rs - DEC_SEQ), (0, 0)))
    zt_s = jnp.transpose(zif[NP:].reshape(DEC_BATCH, DEC_SEQ, 2 * MLSTM_HEADS), (0, 2, 1))
    pad_i = jnp.full((DEC_BATCH, MLSTM_HEADS, ls - DEC_SEQ), NEG, F32)
    zt_s = jnp.concatenate([zt_s, jnp.concatenate([pad_i, -pad_i], axis=1)], axis=2)
    hm_s, *s_new = _mlstm(mz_s.reshape(DEC_BATCH * rs, 4 * MLSTM_W), zt_s, bias, gain,
                          (state_C, state_n, state_m[..., None]), l, s_prev, l,
                          nb=DEC_BATCH, n_chunks=1, rows=rs, L=ls, out_rows=DEC_BATCH * rs)
    hm_s = hm_s.reshape(DEC_BATCH, rs, MLSTM_W)[:, :DEC_SEQ].reshape(NS, MLSTM_W)
    hm = hm_p.at[NP:].set(hm_s)

    zg = _matmul(h, wg, tm=2 * tm, tn=1024, out_dtype=BF16, name="branch_gates")
    merged = _merge((o_attn, pm, hm), [w.astype(BF16) for w in (w_proj_attn, w_proj_pool, w_proj_mlstm)], zg, l)
    x = _matmul(merged, w_out.astype(BF16), layer=l, tm=2 * tm, tn=512, out_dtype=F32, name="out_proj",
                epilogue=_add_residual, extras=[(x, pl.BlockSpec((2 * tm, 512), lambda i, j, k: (i, j)))])

    hf = _rmsnorm(x, g_mlp[l], BF16)
    a = _matmul(hf, w_up.astype(BF16), layer=l, tm=2 * tm, tn=1024, out_dtype=BF16, name="mlp_up", epilogue=_relu2)
    x = _matmul(a, w_down.astype(BF16), layer=l, tm=tm // 2, tn=512, out_dtype=F32, name="mlp_down",
                epilogue=_add_residual, extras=[(x, pl.BlockSpec((tm // 2, 512), lambda i, j, k: (i, j)))])

    p_state = (ckv[:NP, :KV_RANK].reshape(BATCH, SEQ, KV_RANK),
               ckv[:NP, KV_RANK:KV_COLS].reshape(BATCH, SEQ, QK_ROPE),
               jnp.stack([u[(b + 1) * SEQ - POOL_CTX:(b + 1) * SEQ] for b in range(BATCH)]))
    s_state = (ckv[NP:, :KV_RANK].reshape(DEC_BATCH, DEC_SEQ, KV_RANK),
               ckv[NP:, KV_RANK:KV_COLS].reshape(DEC_BATCH, DEC_SEQ, QK_ROPE),
               ctx_s[:, -POOL_CTX:])
    return x, p_state, s_state, (p_new, s_new)


def kernel(x_prompt, x_sample, cache_latent, cache_krope, page_table, state_C, state_n, state_m, state_pool,
           g_mix, w_in, kv_norm_g, w_ukv, w_pool, pool_scale, b_i, b_f, mlstm_norm_g,
           w_proj_attn, w_proj_pool, w_proj_mlstm, w_out, g_mlp, w_up, w_down, g_final):
    x = jnp.concatenate([x_prompt.reshape(NP, D_MODEL), x_sample.reshape(NS, D_MODEL)], axis=0).astype(F32)
    rope = _rope_tables()
    cache_krope_t = jnp.swapaxes(cache_krope, 2, 3)
    p_states, s_states, mstate = [], [], None
    for l in range(DEPTH):
        x, ps, ss, mstate = _layer(x, l, rope, mstate, cache_latent, cache_krope_t, page_table, state_C, state_n,
                                   state_m, state_pool, g_mix, w_in, kv_norm_g, w_ukv, w_pool, pool_scale, b_i, b_f,
                                   mlstm_norm_g, w_proj_attn, w_proj_pool, w_proj_mlstm, w_out, g_mlp, w_up, w_down)
        p_states.append(ps)
        s_states.append(ss)
    y_prompt = _rmsnorm(x, g_final, F32, tm=512, rows=NP).reshape(BATCH, SEQ, D_MODEL).astype(x_prompt.dtype)
    y_sample = _rmsnorm(x[NP:], g_final, F32, tm=NS).reshape(DEC_BATCH, DEC_SEQ, D_MODEL).astype(x_sample.dtype)

    def stack(states, i, like):
        return jnp.stack([s[i] for s in states]).astype(like.dtype)

    (p_c, p_n, p_m), (s_c, s_n, s_m) = mstate
    return (y_prompt, y_sample,
            stack(p_states, 0, cache_latent), stack(p_states, 1, cache_krope),
            p_c.astype(state_C.dtype), p_n.astype(state_n.dtype), p_m[..., 0].astype(state_m.dtype),
            stack(p_states, 2, state_pool),
            stack(s_states, 0, cache_latent), stack(s_states, 1, cache_krope),
            s_c.astype(state_C.dtype), s_n.astype(state_n.dtype), s_m[..., 0].astype(state_m.dtype),
            stack(s_states, 2, state_pool))
```
